```python
import math
import jax, jax.numpy as jnp
from jax import lax
import numpy as np

D_MODEL = 4096
BATCH = 2
SEQ = 8192
DEPTH = 2

CTX_LEN = 256
GRID_W = 64

MLA_HEADS = 32
QK_NOPE = 128
QK_ROPE = 64
V_DIM = 128
Q_RANK = 768
KV_RANK = 512
MLA_WIDTH = MLA_HEADS * V_DIM
QK_DIM = QK_NOPE + QK_ROPE
ATTN_SCALE = 1.0 / math.sqrt(QK_DIM)
ROPE_BASE = 10000.0
Q_BLOCK = 128

SSM_INNER = 2 * D_MODEL
SSM_HEADDIM = 64
SSM_HEADS = SSM_INNER // SSM_HEADDIM
SSM_GROUPS = 8
SSM_HPG = SSM_HEADS // SSM_GROUPS
SSM_STATE = 128
CONV_K = 5
CONV_CH = SSM_INNER + 2 * SSM_GROUPS * SSM_STATE
CHUNK = 128

EPS = 1e-6

IN_WIDTH = (Q_RANK + KV_RANK + QK_ROPE + MLA_WIDTH + 2 * SSM_INNER
            + 2 * SSM_GROUPS * SSM_STATE + 2 * SSM_HEADS + 2 * D_MODEL)

kernel_name = "hybrid_mla_ssd_parallel_gated_dit"


def _split_in(p):
    sizes = (Q_RANK, KV_RANK, QK_ROPE, MLA_WIDTH, SSM_INNER, SSM_INNER,
             SSM_GROUPS * SSM_STATE, SSM_GROUPS * SSM_STATE, SSM_HEADS, SSM_HEADS,
             D_MODEL, D_MODEL)
    offs, acc = [], 0
    for s in sizes[:-1]:
        acc += s
        offs.append(acc)
    return jnp.split(p, offs, axis=-1)


def rms_norm(x, g):
    xf = x.astype(jnp.float32)
    y = xf * lax.rsqrt(jnp.mean(xf * xf, axis=-1, keepdims=True) + EPS)
    return (y * g.astype(jnp.float32)).astype(x.dtype)


def axial_angles(n_tokens):
    n_rows = n_tokens // GRID_W
    rows = jnp.broadcast_to(jnp.arange(n_rows, dtype=jnp.float32)[:, None], (n_rows, GRID_W)).reshape(-1)
    cols = jnp.broadcast_to(jnp.arange(GRID_W, dtype=jnp.float32)[None, :], (n_rows, GRID_W)).reshape(-1)
    n_freq = QK_ROPE // 4
    inv = ROPE_BASE ** (-jnp.arange(n_freq, dtype=jnp.float32) / n_freq)
    return jnp.concatenate([rows[:, None] * inv, cols[:, None] * inv], axis=-1)


def apply_rope(x, ang):
    x1, x2 = jnp.split(x, 2, axis=-1)
    cos = jnp.cos(ang).astype(x.dtype)
    sin = jnp.sin(ang).astype(x.dtype)
    return jnp.concatenate([x1 * cos - x2 * sin, x1 * sin + x2 * cos], axis=-1)


def mla_up(cq, ckv, g_q, w_uq, g_kv, w_ukv):
    b, n = cq.shape[:2]
    q = (rms_norm(cq, g_q) @ w_uq).reshape(b, n, MLA_HEADS, QK_DIM)
    kv = (rms_norm(ckv, g_kv) @ w_ukv).reshape(b, n, MLA_HEADS, QK_NOPE + V_DIM)
    return q[..., :QK_NOPE], q[..., QK_NOPE:], kv[..., :QK_NOPE], kv[..., QK_NOPE:]


def attend(q, k, v):
    s = jnp.einsum('bqhd,bkhd->bhqk', q, k).astype(jnp.float32) * ATTN_SCALE
    p = jax.nn.softmax(s, axis=-1).astype(v.dtype)
    return jnp.einsum('bhqk,bkhd->bqhd', p, v)


def attend_blocked(q, k, v):
    b, n, h, d = q.shape
    qb = jnp.moveaxis(q.reshape(b, n // Q_BLOCK, Q_BLOCK, h, d), 1, 0)
    ob = lax.map(lambda qblk: attend(qblk, k, v), qb)
    return jnp.moveaxis(ob, 0, 1).reshape(b, n, h, v.shape[-1])


def dwconv_centred(u, w, bias):
    pad = (CONV_K - 1) // 2
    y = lax.conv_general_dilated(u, w[:, None, :].astype(u.dtype), window_strides=(1,),
                                 padding=[(pad, pad)], dimension_numbers=('NWC', 'WIO', 'NWC'),
                                 feature_group_count=u.shape[-1])
    return jax.nn.silu(y + bias.astype(u.dtype))


def ssd_scan(xh, dt, A, Bm, Cm, h0):
    b, n = xh.shape[:2]
    nc = n // CHUNK

    def chunks(t):
        return jnp.moveaxis(t.reshape(b, nc, CHUNK, *t.shape[2:]), 1, 0)

    idx = jnp.arange(CHUNK)
    lower = (idx[:, None] >= idx[None, :])[None, :, :, None, None]

    def step(h, inp):
        xc, dtc, bc, cc = inp
        acs = jnp.cumsum(dtc * A, axis=1)
        seg = acs[:, :, None] - acs[:, None, :]
        lmat = jnp.exp(jnp.where(lower, seg, -jnp.inf))
        xdt = xc * dtc[..., None]
        cb = jnp.einsum('blgn,bsgn->bgls', cc, bc)
        y_diag = jnp.einsum('bgls,blsgh,bsghp->blghp', cb, lmat, xdt)
        y_off = jnp.einsum('blgn,bghpn->blghp', cc, h) * jnp.exp(acs)[..., None]
        tail = jnp.exp(acs[:, -1:] - acs)
        h_new = (h * jnp.exp(acs[:, -1])[..., None, None]
                 + jnp.einsum('bsgn,bsghp->bghpn', bc, xdt * tail[..., None]))
        return h_new, y_diag + y_off

    h_fin, ys = lax.scan(step, h0, (chunks(xh), chunks(dt), chunks(Bm), chunks(Cm)))
    return jnp.moveaxis(ys, 0, 1).reshape(xh.shape), h_fin


def ssm_branch(z, xs, bs, cs, dtf_raw, dtb_raw, conv_w, conv_b, dt_bias_f, dt_bias_b,
               a_log_f, a_log_b, d_skip, g_ssm, h0f, h0b):
    b, n = z.shape[:2]
    xbc = dwconv_centred(jnp.concatenate([xs, bs, cs], axis=-1), conv_w, conv_b)
    xs = xbc[..., :SSM_INNER]
    bs = xbc[..., SSM_INNER:SSM_INNER + SSM_GROUPS * SSM_STATE]
    cs = xbc[..., SSM_INNER + SSM_GROUPS * SSM_STATE:]
    f32 = jnp.float32
    xh = xs.astype(f32).reshape(b, n, SSM_GROUPS, SSM_HPG, SSM_HEADDIM)
    bm = bs.astype(f32).reshape(b, n, SSM_GROUPS, SSM_STATE)
    cm = cs.astype(f32).reshape(b, n, SSM_GROUPS, SSM_STATE)
    dt_f = jax.nn.softplus(dtf_raw.astype(f32) + dt_bias_f.astype(f32)).reshape(b, n, SSM_GROUPS, SSM_HPG)
    dt_b = jax.nn.softplus(dtb_raw.astype(f32) + dt_bias_b.astype(f32)).reshape(b, n, SSM_GROUPS, SSM_HPG)
    a_f = -jnp.exp(a_log_f.astype(f32)).reshape(SSM_GROUPS, SSM_HPG)
    a_b = -jnp.exp(a_log_b.astype(f32)).reshape(SSM_GROUPS, SSM_HPG)
    y_f, hf = ssd_scan(xh, dt_f, a_f, bm, cm, h0f)
    flip = lambda t: jnp.flip(t, axis=1)
    y_b, hb = ssd_scan(flip(xh), flip(dt_b), a_b, flip(bm), flip(cm), h0b)
    y = y_f + flip(y_b) + xh * d_skip.astype(f32).reshape(SSM_GROUPS, SSM_HPG)[..., None]
    y = y.reshape(b, n, SSM_INNER).astype(z.dtype)
    return rms_norm(y * jax.nn.silu(z), g_ssm), hf, hb


def setup_inputs(seed: int = 0) -> dict:
    key = jax.random.key(seed)
    ks = jax.random.split(key, 32)
    f32 = jnp.float32

    def nrm(k, shape, scale):
        return jax.random.normal(k, shape, f32) * scale

    def gain(k, shape):
        return 1.0 + 0.1 * jax.random.normal(k, shape, f32)

    def dt_bias(k):
        u = jax.random.uniform(k, (DEPTH, SSM_HEADS), f32)
        dt0 = jnp.exp(u * (math.log(0.1) - math.log(0.001)) + math.log(0.001))
        return dt0 + jnp.log(-jnp.expm1(-dt0))

    def a_log(k):
        return jnp.log(jax.random.uniform(k, (DEPTH, SSM_HEADS), f32, 1.0, 16.0))

    return {
        "x": nrm(ks[0], (BATCH, SEQ, D_MODEL), 1.0),
        "c": nrm(ks[1], (BATCH, D_MODEL), 1.0),
        "ctx": nrm(ks[2], (BATCH, CTX_LEN, D_MODEL), 1.0),
        "c_ctx": nrm(ks[3], (D_MODEL,), 1.0),
        "w_ada": nrm(ks[4], (DEPTH, D_MODEL, 3 * D_MODEL), 0.5 * D_MODEL ** -0.5),
        "b_ada": nrm(ks[5], (DEPTH, 3 * D_MODEL), 0.02),
        "g_pre": gain(ks[6], (DEPTH, D_MODEL)),
        "w_in": nrm(ks[7], (DEPTH, D_MODEL, IN_WIDTH), D_MODEL ** -0.5),
        "g_q": gain(ks[8], (DEPTH, Q_RANK)),
        "w_uq": nrm(ks[9], (DEPTH, Q_RANK, MLA_HEADS * QK_DIM), Q_RANK ** -0.5),
        "g_kv": gain(ks[10], (DEPTH, KV_RANK)),
        "w_ukv": nrm(ks[11], (DEPTH, KV_RANK, MLA_HEADS * (QK_NOPE + V_DIM)), KV_RANK ** -0.5),
        "conv_w": nrm(ks[12], (DEPTH, CONV_K, CONV_CH), CONV_K ** -0.5),
        "conv_b": nrm(ks[13], (DEPTH, CONV_CH), 0.02),
        "dt_bias_f": dt_bias(ks[14]),
        "dt_bias_b": dt_bias(ks[15]),
        "a_log_f": a_log(ks[16]),
        "a_log_b": a_log(ks[17]),
        "d_skip": gain(ks[18], (DEPTH, SSM_HEADS)),
        "g_ssm": gain(ks[19], (DEPTH, SSM_INNER)),
        "w_proj_a": nrm(ks[20], (DEPTH, MLA_WIDTH, D_MODEL), MLA_WIDTH ** -0.5),
        "w_proj_b": nrm(ks[21], (DEPTH, SSM_INNER, D_MODEL), SSM_INNER ** -0.5),
        "w_out": nrm(ks[22], (DEPTH, D_MODEL, D_MODEL), D_MODEL ** -0.5),
        "g_final": gain(ks[23], (D_MODEL,)),
    }


def reference(x, c, ctx, c_ctx, w_ada, b_ada, g_pre, w_in, g_q, w_uq, g_kv, w_ukv,
              conv_w, conv_b, dt_bias_f, dt_bias_b, a_log_f, a_log_b, d_skip, g_ssm,
              w_proj_a, w_proj_b, w_out, g_final):
    b, n = x.shape[:2]
    n_ctx = ctx.shape[1]
    ang = axial_angles(n)
    h, h_ctx = x, ctx
    for i in range(DEPTH):
        need_ctx_out = i < DEPTH - 1
        mod = jax.nn.silu(c) @ w_ada[i] + b_ada[i]
        shift, scale, gate = [m[:, None, :] for m in jnp.split(mod, 3, axis=-1)]
        mod_c = jax.nn.silu(c_ctx) @ w_ada[i] + b_ada[i]
        shift_c, scale_c, gate_c = jnp.split(mod_c, 3, axis=-1)

        u = rms_norm(h, g_pre[i]) * (1.0 + scale) + shift
        u_c = rms_norm(h_ctx, g_pre[i]) * (1.0 + scale_c) + shift_c
        (cq, ckv, kpe, ga, z, xs, bs, cs, dtf, dtb, mg_a, mg_b) = _split_in(u @ w_in[i])
        (cq_c, ckv_c, kpe_c, ga_c, z_c, xs_c, bs_c, cs_c, dtf_c, dtb_c, mg_a_c, mg_b_c) = _split_in(u_c @ w_in[i])

        q_nope, q_pe, k_nope, v = mla_up(cq, ckv, g_q[i], w_uq[i], g_kv[i], w_ukv[i])
        q_pe = apply_rope(q_pe, ang[None, :, None, :])
        k_pe = apply_rope(kpe, ang[None])
        q_lat = jnp.concatenate([q_nope, q_pe], axis=-1)
        k_lat = jnp.concatenate([k_nope, jnp.broadcast_to(k_pe[:, :, None, :], (b, n, MLA_HEADS, QK_ROPE))], axis=-1)
        qn_c, qp_c, kn_c, v_c = mla_up(cq_c, ckv_c, g_q[i], w_uq[i], g_kv[i], w_ukv[i])
        k_ctx = jnp.concatenate([kn_c, jnp.broadcast_to(kpe_c[:, :, None, :], (b, n_ctx, MLA_HEADS, QK_ROPE))], axis=-1)
        k_all = jnp.concatenate([k_lat, k_ctx], axis=1)
        v_all = jnp.concatenate([v, v_c], axis=1)
        o_a = attend_blocked(q_lat, k_all, v_all).reshape(b, n, MLA_WIDTH)
        br_a = (o_a * jax.nn.silu(ga)) @ w_proj_a[i]

        h0 = jnp.zeros((b, SSM_GROUPS, SSM_HPG, SSM_HEADDIM, SSM_STATE), jnp.float32)
        ssm_p = (conv_w[i], conv_b[i], dt_bias_f[i], dt_bias_b[i], a_log_f[i], a_log_b[i], d_skip[i], g_ssm[i])
        y_c, hf_c, hb_c = ssm_branch(z_c, xs_c, bs_c, cs_c, dtf_c, dtb_c, *ssm_p, h0, h0)
        y_l, _, _ = ssm_branch(z, xs, bs, cs, dtf, dtb, *ssm_p, hf_c, hb_c)
        br_b = y_l @ w_proj_b[i]

        merged = jax.nn.sigmoid(mg_a) * br_a + jax.nn.sigmoid(mg_b) * br_b
        h_new = h + gate * (merged @ w_out[i])

        if need_ctx_out:
            q_c = jnp.concatenate([qn_c, qp_c], axis=-1)
            o_c = attend(q_c, k_ctx, v_c).reshape(b, n_ctx, MLA_WIDTH)
            br_a_c = (o_c * jax.nn.silu(ga_c)) @ w_proj_a[i]
            br_b_c = y_c @ w_proj_b[i]
            merged_c = jax.nn.sigmoid(mg_a_c) * br_a_c + jax.nn.sigmoid(mg_b_c) * br_b_c
            h_ctx = h_ctx + gate_c * (merged_c @ w_out[i])
        h = h_new
    return rms_norm(h, g_final)
```

```python
import functools
import math

import jax
import jax.numpy as jnp
from jax import lax
from jax.experimental import pallas as pl
from jax.experimental.pallas import tpu as pltpu

F32 = jnp.float32
MXU_DTYPE = jnp.bfloat16

D_MODEL = 4096
GRID_W = 64
MLA_HEADS = 32
QK_NOPE = 128
QK_ROPE = 64
V_DIM = 128
Q_RANK = 768
KV_RANK = 512
MLA_WIDTH = MLA_HEADS * V_DIM
QK_DIM = QK_NOPE + QK_ROPE
ATTN_SCALE = 1.0 / math.sqrt(QK_DIM)
ROPE_BASE = 10000.0
SSM_INNER = 2 * D_MODEL
SSM_HEADDIM = 64
SSM_HEADS = SSM_INNER // SSM_HEADDIM
SSM_GROUPS = 8
SSM_HPG = SSM_HEADS // SSM_GROUPS
SSM_STATE = 128
CONV_K = 5
CONV_CH = SSM_INNER + 2 * SSM_GROUPS * SSM_STATE
CHUNK = 128
EPS = 1e-6

LANES = 128
SUBLANES = 8
VMEM_LIMIT_BYTES = 56 * 1024 * 1024

QK_PAD = 2 * LANES
GROUP_W = SSM_HPG * SSM_HEADDIM
PAIR_W = 2 * SSM_HEADDIM

OFF_Z = 0
OFF_X = OFF_Z + SSM_INNER
OFF_B = OFF_X + SSM_INNER
OFF_C = OFF_B + SSM_GROUPS * SSM_STATE
OFF_GA = OFF_X + CONV_CH
OFF_MGA = OFF_GA + MLA_WIDTH
OFF_MGB = OFF_MGA + D_MODEL
OFF_CKV = OFF_MGB + D_MODEL
OFF_DT = OFF_CKV + KV_RANK
OFF_CQ = OFF_DT + 2 * SSM_HEADS
OFF_KPE = OFF_CQ + Q_RANK
OFF_KPESW = OFF_KPE + LANES
IN_PAD = 32768
assert OFF_KPESW + LANES <= IN_PAD
assert OFF_CQ % Q_RANK == 0 and OFF_CKV % KV_RANK == 0 and OFF_DT % (2 * SSM_HEADS) == 0


def _sigmoid(x):
    return 1.0 / (1.0 + jnp.exp(-x))


def _silu(x):
    return x * _sigmoid(x)


def _softplus(x):
    return jnp.maximum(x, 0.0) + jnp.log1p(jnp.exp(-jnp.abs(x)))


def _dot(a, b):
    return jnp.dot(a, b, preferred_element_type=F32)


def _dot_nt(a, b):
    return lax.dot_general(a, b, (((1,), (1,)), ((), ())), preferred_element_type=F32)


def _split_hi_lo(x):
    hi = x.astype(MXU_DTYPE)
    lo = (x - hi.astype(F32)).astype(MXU_DTYPE)
    return hi, lo


def _row_tile(rows, target):
    best = None
    for t in range(16, min(rows, target) + 1, 16):
        if rows % t == 0:
            best = t
    assert best is not None, rows
    return best


def _params(*sem):
    return pltpu.CompilerParams(dimension_semantics=sem, vmem_limit_bytes=VMEM_LIMIT_BYTES)


def _ada_kernel(c_ref, w_ref, b_ref, o_ref):
    s = _silu(c_ref[...]).astype(MXU_DTYPE)
    o_ref[...] = _dot(s, w_ref[...].astype(MXU_DTYPE)) + b_ref[...]


def _ada(cc, w, b):
    rows, d = cc.shape
    n = w.shape[1]
    tn = 512
    return pl.pallas_call(
        _ada_kernel,
        grid=(n // tn,),
        in_specs=[pl.BlockSpec((rows, d), lambda j: (0, 0)),
                  pl.BlockSpec((d, tn), lambda j: (0, j)),
                  pl.BlockSpec((1, tn), lambda j: (0, j))],
        out_specs=pl.BlockSpec((rows, tn), lambda j: (0, j)),
        out_shape=jax.ShapeDtypeStruct((rows, n), F32),
        compiler_params=_params("arbitrary"),
        name="ada_mod",
    )(cc, w, b.reshape(1, n))


def _norm_mod_kernel(h_ref, g_ref, sh_ref, sc_ref, o_ref):
    x = h_ref[...]
    ms = jnp.mean(x * x, axis=-1, keepdims=True)
    y = x * lax.rsqrt(ms + EPS) * g_ref[...]
    o_ref[...] = (y * (1.0 + sc_ref[...]) + sh_ref[...]).astype(o_ref.dtype)


def _norm_mod(hs, g, mod3, n_lat):
    bsz, rows, d = hs.shape
    tr = 256
    lat_tiles = n_lat // tr
    ctx_row = bsz

    def mod_row(b, i):
        return jnp.where(i < lat_tiles, b, ctx_row)

    return pl.pallas_call(
        _norm_mod_kernel,
        grid=(bsz, rows // tr),
        in_specs=[pl.BlockSpec((None, tr, d), lambda b, i: (b, i, 0)),
                  pl.BlockSpec((1, d), lambda b, i: (0, 0)),
                  pl.BlockSpec((None, 1, d), lambda b, i: (mod_row(b, i), 0, 0)),
                  pl.BlockSpec((None, 1, d), lambda b, i: (mod_row(b, i), 0, 1))],
        out_specs=pl.BlockSpec((None, tr, d), lambda b, i: (b, i, 0)),
        out_shape=jax.ShapeDtypeStruct((bsz, rows, d), MXU_DTYPE),
        compiler_params=_params("parallel", "arbitrary"),
        name="norm_mod",
    )(hs, g.reshape(1, d), mod3, mod3)


def _final_norm_kernel(h_ref, g_ref, o_ref):
    x = h_ref[...]
    ms = jnp.mean(x * x, axis=-1, keepdims=True)
    o_ref[...] = x * lax.rsqrt(ms + EPS) * g_ref[...]


def _final_norm(hs, g, n_lat):
    bsz, _, d = hs.shape
    tr = 256
    return pl.pallas_call(
        _final_norm_kernel,
        grid=(bsz, n_lat // tr),
        in_specs=[pl.BlockSpec((None, tr, d), lambda b, i: (b, i, 0)),
                  pl.BlockSpec((1, d), lambda b, i: (0, 0))],
        out_specs=pl.BlockSpec((None, tr, d), lambda b, i: (b, i, 0)),
        out_shape=jax.ShapeDtypeStruct((bsz, n_lat, d), F32),
        compiler_params=_params("parallel", "arbitrary"),
        name="final_norm",
    )(hs, g.reshape(1, d))


def _mm_kernel(*refs, n_extra, epilogue):
    x_ref, w_ref = refs[:2]
    extras = refs[2:2 + n_extra]
    o_ref = refs[2 + n_extra]
    acc = _dot(x_ref[...], w_ref[...])
    if epilogue is not None:
        acc = epilogue(acc, *extras)
    o_ref[...] = acc.astype(o_ref.dtype)


def _matmul(x3, w, *, rows, tm, tn, out_dtype, epilogue=None, extras=(), name):
    bsz, r_all, k = x3.shape
    n = w.shape[1]
    assert rows % tm == 0 and n % tn == 0
    in_specs = [pl.BlockSpec((None, tm, k), lambda b, i, j: (b, i, 0)),
                pl.BlockSpec((k, tn), lambda b, i, j: (0, j))]
    in_specs += [spec for _, spec in extras]
    return pl.pallas_call(
        functools.partial(_mm_kernel, n_extra=len(extras), epilogue=epilogue),
        grid=(bsz, rows // tm, n // tn),
        in_specs=in_specs,
        out_specs=pl.BlockSpec((None, tm, tn), lambda b, i, j: (b, i, j)),
        out_shape=jax.ShapeDtypeStruct((bsz, r_all, n), out_dtype),
        compiler_params=_params("parallel", "parallel", "arbitrary"),
        name=name,
    )(x3, w, *[a for a, _ in extras])


def _merge_epilogue(acc_b, bra_ref, mga_ref, mgb_ref):
    return _sigmoid(mga_ref[...]) * bra_ref[...] + _sigmoid(mgb_ref[...]) * acc_b


def _residual_epilogue(acc, h_ref, gate_ref, gate_c_ref, *, tm, n_lat):
    row = pl.program_id(1) * tm + lax.broadcasted_iota(jnp.int32, (tm, 1), 0)
    gate = jnp.where(row < n_lat, gate_ref[...], gate_c_ref[...])
    return h_ref[...] + gate * acc


Q_HEADS_PER_TILE = 4


def _q_kernel(cq_ref, g_ref, wa_ref, wb_ref, cos_ref, sin_ref, o_ref, xn_scr):
    @pl.when(pl.program_id(2) == 0)
    def _():
        x = cq_ref[...]
        ms = jnp.mean(x * x, axis=-1, keepdims=True)
        xn_scr[...] = (x * lax.rsqrt(ms + EPS) * g_ref[...]).astype(xn_scr.dtype)

    xn = xn_scr[...]
    a = _dot(xn, wa_ref[...])
    sw = _dot(xn, wb_ref[...])
    cos = cos_ref[...]
    sin = sin_ref[...]
    for hh in range(Q_HEADS_PER_TILE):
        base = hh * QK_PAD
        o_ref[:, base:base + LANES] = (a[:, base:base + LANES] * ATTN_SCALE).astype(o_ref.dtype)
        rope = a[:, base + LANES:base + QK_PAD] * cos + sw[:, hh * LANES:(hh + 1) * LANES] * sin
        o_ref[:, base + LANES:base + QK_PAD] = (rope * ATTN_SCALE).astype(o_ref.dtype)


def _q_proj(p, g_q, wqa, wqb, cos_t, sin_t, tm):
    bsz, rows, _ = p.shape
    tn = Q_HEADS_PER_TILE * QK_PAD
    n = MLA_HEADS * QK_PAD
    return pl.pallas_call(
        _q_kernel,
        grid=(bsz, rows // tm, n // tn),
        in_specs=[pl.BlockSpec((None, tm, Q_RANK), lambda b, i, j: (b, i, OFF_CQ // Q_RANK)),
                  pl.BlockSpec((1, Q_RANK), lambda b, i, j: (0, 0)),
                  pl.BlockSpec((Q_RANK, tn), lambda b, i, j: (0, j)),
                  pl.BlockSpec((Q_RANK, tn // 2), lambda b, i, j: (0, j)),
                  pl.BlockSpec((tm, LANES), lambda b, i, j: (i, 0)),
                  pl.BlockSpec((tm, LANES), lambda b, i, j: (i, 0))],
        out_specs=pl.BlockSpec((None, tm, tn), lambda b, i, j: (b, i, j)),
        out_shape=jax.ShapeDtypeStruct((bsz, rows, n), MXU_DTYPE),
        scratch_shapes=[pltpu.VMEM((tm, Q_RANK), MXU_DTYPE)],
        compiler_params=_params("parallel", "parallel", "arbitrary"),
        name="q_up",
    )(p, g_q.reshape(1, Q_RANK), wqa, wqb, cos_t, sin_t)


def _kv_kernel(ckv_ref, g_ref, wk_ref, wv_ref, kpe_ref, kpesw_ref, cos_ref, sin_ref,
               k_ref, v_ref, xn_scr, kp_scr):
    @pl.when(pl.program_id(2) == 0)
    def _():
        x = ckv_ref[...]
        ms = jnp.mean(x * x, axis=-1, keepdims=True)
        xn_scr[...] = (x * lax.rsqrt(ms + EPS) * g_ref[...]).astype(xn_scr.dtype)
        kp_scr[...] = (kpe_ref[...] * cos_ref[...] + kpesw_ref[...] * sin_ref[...]).astype(kp_scr.dtype)

    xn = xn_scr[...]
    kn = _dot(xn, wk_ref[...])
    v_ref[...] = _dot(xn, wv_ref[...]).astype(v_ref.dtype)
    for hh in range(Q_HEADS_PER_TILE):
        base = hh * QK_PAD
        k_ref[:, base:base + LANES] = kn[:, hh * LANES:(hh + 1) * LANES].astype(k_ref.dtype)
        k_ref[:, base + LANES:base + QK_PAD] = kp_scr[...]


def _kv_proj(p, g_kv, wk, wv, cos_t, sin_t, tm):
    bsz, rows, _ = p.shape
    tk = Q_HEADS_PER_TILE * QK_PAD
    tv = Q_HEADS_PER_TILE * V_DIM
    nk = MLA_HEADS * QK_PAD
    nv = MLA_HEADS * V_DIM
    return pl.pallas_call(
        _kv_kernel,
        grid=(bsz, rows // tm, nk // tk),
        in_specs=[pl.BlockSpec((None, tm, KV_RANK), lambda b, i, j: (b, i, OFF_CKV // KV_RANK)),
                  pl.BlockSpec((1, KV_RANK), lambda b, i, j: (0, 0)),
                  pl.BlockSpec((KV_RANK, tv), lambda b, i, j: (0, j)),
                  pl.BlockSpec((KV_RANK, tv), lambda b, i, j: (0, j)),
                  pl.BlockSpec((None, tm, LANES), lambda b, i, j: (b, i, OFF_KPE // LANES)),
                  pl.BlockSpec((None, tm, LANES), lambda b, i, j: (b, i, OFF_KPESW // LANES)),
                  pl.BlockSpec((tm, LANES), lambda b, i, j: (i, 0)),
                  pl.BlockSpec((tm, LANES), lambda b, i, j: (i, 0))],
        out_specs=[pl.BlockSpec((None, tm, tk), lambda b, i, j: (b, i, j)),
                   pl.BlockSpec((None, tm, tv), lambda b, i, j: (b, i, j))],
        out_shape=[jax.ShapeDtypeStruct((bsz, rows, nk), MXU_DTYPE),
                   jax.ShapeDtypeStruct((bsz, rows, nv), MXU_DTYPE)],
        scratch_shapes=[pltpu.VMEM((tm, KV_RANK), MXU_DTYPE), pltpu.VMEM((tm, LANES), MXU_DTYPE)],
        compiler_params=_params("parallel", "parallel", "arbitrary"),
        name="kv_up",
    )(p, g_kv.reshape(1, KV_RANK), wk, wv, p, p, cos_t, sin_t)


ATT_TQ = 256
ATT_TK = 256


def _attn_kernel(q_ref, k_ref, v_ref, ga_ref, o_ref, *, lat_tiles, all_tiles):
    qi = pl.program_id(2)
    q = q_ref[...]
    first = jnp.where(qi < lat_tiles, 0, lat_tiles)

    def body(t, carry):
        m, l, acc = carry
        r0 = pl.multiple_of(t * ATT_TK, ATT_TK)
        k = k_ref[pl.ds(r0, ATT_TK), :]
        v = v_ref[pl.ds(r0, ATT_TK), :]
        s = _dot_nt(q, k)
        m_new = jnp.maximum(m, jnp.max(s, axis=-1, keepdims=True))
        alpha = jnp.exp(m - m_new)
        e = jnp.exp(s - m_new)
        l = alpha * l + jnp.sum(e, axis=-1, keepdims=True)
        acc = alpha * acc + _dot(e.astype(v.dtype), v)
        return m_new, l, acc

    init = (jnp.full((ATT_TQ, 1), -jnp.inf, F32), jnp.zeros((ATT_TQ, 1), F32),
            jnp.zeros((ATT_TQ, V_DIM), F32))
    _, l, acc = lax.fori_loop(first, all_tiles, body, init)
    o_ref[...] = ((acc / l) * _silu(ga_ref[...])).astype(o_ref.dtype)


def _attention(q, k, v, p, *, n_lat, q_rows):
    bsz, rows, _ = q.shape
    lat_tiles = n_lat // ATT_TQ
    all_tiles = rows // ATT_TK
    kern = functools.partial(_attn_kernel, lat_tiles=lat_tiles, all_tiles=all_tiles)
    return pl.pallas_call(
        kern,
        grid=(bsz, MLA_HEADS, q_rows // ATT_TQ),
        in_specs=[pl.BlockSpec((None, ATT_TQ, QK_PAD), lambda b, h, i: (b, i, h)),
                  pl.BlockSpec((None, rows, QK_PAD), lambda b, h, i: (b, 0, h)),
                  pl.BlockSpec((None, rows, V_DIM), lambda b, h, i: (b, 0, h)),
                  pl.BlockSpec((None, ATT_TQ, V_DIM), lambda b, h, i: (b, i, OFF_GA // V_DIM + h))],
        out_specs=pl.BlockSpec((None, ATT_TQ, V_DIM), lambda b, h, i: (b, i, h)),
        out_shape=jax.ShapeDtypeStruct((bsz, rows, MLA_WIDTH), MXU_DTYPE),
        compiler_params=_params("parallel", "parallel", "arbitrary"),
        name="attention",
    )(q, k, v, p)


CONV_TC = 1024
HALO = SUBLANES


def _conv_kernel(prev_ref, cur_ref, next_ref, w_ref, b_ref, o_ref, ext, *, lat_chunks, all_chunks):
    c = pl.program_id(1)
    has_left = jnp.logical_and(c != 0, c != lat_chunks)
    has_right = jnp.logical_and(c != lat_chunks - 1, c != all_chunks - 1)
    ext[0:HALO, :] = jnp.where(has_left, prev_ref[...], 0.0)
    ext[HALO:HALO + CHUNK, :] = cur_ref[...]
    ext[HALO + CHUNK:, :] = jnp.where(has_right, next_ref[...], 0.0)
    pad = (CONV_K - 1) // 2
    acc = jnp.broadcast_to(b_ref[...], (CHUNK, CONV_TC))
    for kk in range(CONV_K):
        lo = HALO - pad + kk
        acc = acc + w_ref[kk:kk + 1, :] * ext[lo:lo + CHUNK, :]
    o_ref[...] = _silu(acc)


def _conv(p, w8, bias, n_lat):
    bsz, rows, _ = p.shape
    all_chunks = rows // CHUNK
    lat_chunks = n_lat // CHUNK
    per = CHUNK // HALO
    cb = OFF_X // CONV_TC
    last_halo = rows // HALO - 1
    kern = functools.partial(_conv_kernel, lat_chunks=lat_chunks, all_chunks=all_chunks)
    return pl.pallas_call(
        kern,
        grid=(bsz, all_chunks, CONV_CH // CONV_TC),
        in_specs=[pl.BlockSpec((None, HALO, CONV_TC),
                               lambda b, c, j: (b, jnp.maximum(c * per - 1, 0), cb + j)),
                  pl.BlockSpec((None, CHUNK, CONV_TC), lambda b, c, j: (b, c, cb + j)),
                  pl.BlockSpec((None, HALO, CONV_TC),
                               lambda b, c, j: (b, jnp.minimum((c + 1) * per, last_halo), cb + j)),
                  pl.BlockSpec((SUBLANES, CONV_TC), lambda b, c, j: (0, j)),
                  pl.BlockSpec((1, CONV_TC), lambda b, c, j: (0, j))],
        out_specs=pl.BlockSpec((None, CHUNK, CONV_TC), lambda b, c, j: (b, c, j)),
        out_shape=jax.ShapeDtypeStruct((bsz, rows, CONV_CH), F32),
        scratch_shapes=[pltpu.VMEM((CHUNK + 2 * HALO, CONV_TC), F32)],
        compiler_params=_params("parallel", "parallel", "arbitrary"),
        name="dwconv",
    )(p, p, p, w8, bias.reshape(1, CONV_CH))


def _ssd_kernel(*refs, reverse):
    if reverse:
        (x_ref, b_ref, c_ref, dt_ref, bias_ref, alog_ref, e2_ref,
         yf_ref, z_ref, dsk_ref, gs_ref, o_ref, state, rrow, ybuf) = refs
        dest = ybuf
    else:
        (x_ref, b_ref, c_ref, dt_ref, bias_ref, alog_ref, e2_ref, o_ref, state, rrow) = refs
        dest = o_ref
    L = CHUNK
    half = PAIR_W // 2

    @pl.when(pl.program_id(1) == 0)
    def _():
        state[...] = jnp.zeros_like(state)

    off = SSM_HEADS if reverse else 0
    dt = _softplus(dt_ref[:, off:off + SSM_HEADS] + bias_ref[...])
    a = dt * (-jnp.exp(alog_ref[...]))

    ri = lax.broadcasted_iota(jnp.int32, (L, L), 0)
    ci = lax.broadcasted_iota(jnp.int32, (L, L), 1)
    tri = jnp.where((ri <= ci) if reverse else (ri >= ci), 1.0, 0.0).astype(MXU_DTYPE)
    p1 = a.astype(MXU_DTYPE)
    r1 = a - p1.astype(F32)
    p2 = r1.astype(MXU_DTYPE)
    p3 = (r1 - p2.astype(F32)).astype(MXU_DTYPE)
    acs = _dot(tri, p1) + _dot(tri, p2) + _dot(tri, p3)

    acs_t = acs.T
    lane_h = lax.broadcasted_iota(jnp.int32, (SSM_HEADS // 2, LANES), 1)
    ev = acs_t[0:SSM_HEADS // 2]
    od = acs_t[SSM_HEADS // 2:]
    rrow[0] = jnp.where(lane_h < half, ev, pltpu.roll(od, half, 1))
    rrow[1] = jnp.where(lane_h < half, pltpu.roll(ev, half, 1), od)

    acs_hl = jnp.concatenate(_split_hi_lo(acs), axis=1)
    dt_hl = jnp.concatenate(_split_hi_lo(dt), axis=1)
    edge = 0 if reverse else L - 1

    li = lax.broadcasted_iota(jnp.int32, (L, LANES), 0)
    lane = lax.broadcasted_iota(jnp.int32, (L, LANES), 1)
    src = jnp.where(lane < half, lane, lane - half)
    masks = []
    for sb in range(2):
        s_pos = src + sb * half
        masks.append((li <= s_pos) if reverse else (li >= s_pos))
    lane_s = lax.broadcasted_iota(jnp.int32, (half, LANES), 1)
    first_head = lane_s < half

    def group(g, carry):
        c0 = pl.multiple_of(g * GROUP_W, GROUP_W)
        n0 = pl.multiple_of(g * SSM_STATE, SSM_STATE)
        e2g = e2_ref[:, pl.ds(c0, GROUP_W)]
        acs_x = _dot(acs_hl, e2g)
        dt_x = _dot(dt_hl, e2g)
        xdt = x_ref[:, pl.ds(c0, GROUP_W)] * dt_x
        bg = b_ref[:, pl.ds(n0, SSM_STATE)]
        cg = c_ref[:, pl.ds(n0, SSM_STATE)].astype(MXU_DTYPE)
        bgb = bg.astype(MXU_DTYPE)
        last_x = acs_x[edge:edge + 1, :]

        sg = state[g]
        y_off = _dot(cg, sg.astype(MXU_DTYPE)) * jnp.exp(acs_x)
        wgt = (xdt * jnp.exp(last_x - acs_x)).astype(MXU_DTYPE)
        state[g] = sg * jnp.exp(last_x) + _dot(bg.T.astype(MXU_DTYPE), wgt)

        cbd = []
        for sb in range(2):
            bs = bgb[sb * half:(sb + 1) * half]
            cbd.append(_dot_nt(cg, jnp.concatenate([bs, bs], axis=0)))

        for jj in range(GROUP_W // PAIR_W):
            col = acs_x[:, jj * PAIR_W:(jj + 1) * PAIR_W]
            yp = y_off[:, jj * PAIR_W:(jj + 1) * PAIR_W]
            for sb in range(2):
                row = rrow[sb, pl.ds(g * (GROUP_W // PAIR_W) + jj, 1), :]
                lm = jnp.exp(jnp.where(masks[sb], col - row, -jnp.inf))
                mp = (cbd[sb] * lm).astype(MXU_DTYPE)
                xs = xdt[sb * half:(sb + 1) * half, jj * PAIR_W:(jj + 1) * PAIR_W]
                xbd = jnp.concatenate([jnp.where(first_head, xs, 0.0),
                                       jnp.where(first_head, 0.0, xs)], axis=0).astype(MXU_DTYPE)
                yp = yp + _dot(mp, xbd)
            dest[:, pl.ds(pl.multiple_of(c0 + jj * PAIR_W, PAIR_W), PAIR_W)] = yp
        return carry

    lax.fori_loop(0, SSM_GROUPS, group, 0)

    if reverse:
        ssq = jnp.zeros((L, 1), F32)
        for g in range(SSM_GROUPS):
            sl = slice(g * GROUP_W, (g + 1) * GROUP_W)
            y = ybuf[:, sl] + yf_ref[:, sl] + x_ref[:, sl] * dsk_ref[:, sl]
            yz = y * _silu(z_ref[:, sl])
            ybuf[:, sl] = yz
            ssq = ssq + jnp.sum(yz * yz, axis=-1, keepdims=True)
        r = lax.rsqrt(ssq / SSM_INNER + EPS)
        for g in range(SSM_GROUPS):
            sl = slice(g * GROUP_W, (g + 1) * GROUP_W)
            o_ref[:, sl] = (ybuf[:, sl] * r * gs_ref[:, sl]).astype(o_ref.dtype)


def _ssd(xbc, p, bias, alog, e2, n_lat, *, reverse, yf=None, dskip_x=None, g_ssm=None):
    bsz, rows, _ = xbc.shape
    all_chunks = rows // CHUNK
    lat_chunks = n_lat // CHUNK
    ctx_chunks = all_chunks - lat_chunks

    if reverse:
        def chunk(s):
            return all_chunks - 1 - s
    else:
        def chunk(s):
            return jnp.where(s < ctx_chunks, lat_chunks + s, s - ctx_chunks)

    row_spec = lambda width, blk: pl.BlockSpec((None, CHUNK, width), lambda b, s: (b, chunk(s), blk))
    const = lambda shape: pl.BlockSpec(shape, lambda b, s: (0,) * len(shape))
    ng = SSM_GROUPS * SSM_STATE
    in_specs = [row_spec(SSM_INNER, 0),
                row_spec(ng, SSM_INNER // ng),
                row_spec(ng, SSM_INNER // ng + 1),
                row_spec(2 * SSM_HEADS, OFF_DT // (2 * SSM_HEADS)),
                const((1, SSM_HEADS)), const((1, SSM_HEADS)), const((2 * SSM_HEADS, SSM_INNER))]
    args = [xbc, xbc, xbc, p, bias.reshape(1, SSM_HEADS), alog.reshape(1, SSM_HEADS), e2]
    scratch = [pltpu.VMEM((SSM_GROUPS, SSM_STATE, GROUP_W), F32),
               pltpu.VMEM((2, SSM_HEADS // 2, LANES), F32)]
    if reverse:
        in_specs += [row_spec(SSM_INNER, 0), row_spec(SSM_INNER, OFF_Z // SSM_INNER),
                     const((1, SSM_INNER)), const((1, SSM_INNER))]
        args += [yf, p, dskip_x.reshape(1, SSM_INNER), g_ssm.reshape(1, SSM_INNER)]
        scratch.append(pltpu.VMEM((CHUNK, SSM_INNER), F32))
        out_dtype = MXU_DTYPE
    else:
        out_dtype = F32
    return pl.pallas_call(
        functools.partial(_ssd_kernel, reverse=reverse),
        grid=(bsz, all_chunks),
        in_specs=in_specs,
        out_specs=row_spec(SSM_INNER, 0),
        out_shape=jax.ShapeDtypeStruct((bsz, rows, SSM_INNER), out_dtype),
        scratch_shapes=scratch,
        compiler_params=_params("parallel", "arbitrary"),
        name="ssd_bwd" if reverse else "ssd_fwd",
    )(*args)


def _head_perm():
    k = jnp.arange(SSM_HEADS)
    return jnp.where(k < SSM_HEADS // 2, 2 * k, 2 * (k - SSM_HEADS // 2) + 1)


def _prep_w_in(w):
    sizes = (Q_RANK, KV_RANK, QK_ROPE, MLA_WIDTH, SSM_INNER, SSM_INNER,
             SSM_GROUPS * SSM_STATE, SSM_GROUPS * SSM_STATE, SSM_HEADS, SSM_HEADS, D_MODEL, D_MODEL)
    offs = [0]
    for s in sizes:
        offs.append(offs[-1] + s)
    (cq, ckv, kpe, ga, z, xs, bs, cs, dtf, dtb, mga, mgb) = [w[:, offs[i]:offs[i + 1]] for i in range(12)]
    perm = _head_perm()
    zpad = jnp.zeros((w.shape[0], LANES - QK_ROPE), w.dtype)
    k1, k2 = kpe[:, :QK_ROPE // 2], kpe[:, QK_ROPE // 2:]
    cols = [z, xs, bs, cs, ga, mga, mgb, ckv, dtf[:, perm], dtb[:, perm], cq,
            kpe, zpad, -k2, k1, zpad]
    used = sum(c.shape[1] for c in cols)
    cols.append(jnp.zeros((w.shape[0], IN_PAD - used), w.dtype))
    return jnp.concatenate(cols, axis=1).astype(MXU_DTYPE)


def _prep_w_uq(w):
    w3 = w.reshape(Q_RANK, MLA_HEADS, QK_DIM)
    nope = w3[..., :QK_NOPE]
    r1 = w3[..., QK_NOPE:QK_NOPE + QK_ROPE // 2]
    r2 = w3[..., QK_NOPE + QK_ROPE // 2:]
    zp = jnp.zeros((Q_RANK, MLA_HEADS, LANES - QK_ROPE), w.dtype)
    wa = jnp.concatenate([nope, r1, r2, zp], axis=-1).reshape(Q_RANK, MLA_HEADS * QK_PAD)
    wb = jnp.concatenate([-r2, r1, zp], axis=-1).reshape(Q_RANK, MLA_HEADS * LANES)
    return wa.astype(MXU_DTYPE), wb.astype(MXU_DTYPE)


def _prep_w_ukv(w):
    w3 = w.reshape(KV_RANK, MLA_HEADS, QK_NOPE + V_DIM)
    wk = w3[..., :QK_NOPE].reshape(KV_RANK, MLA_HEADS * QK_NOPE)
    wv = w3[..., QK_NOPE:].reshape(KV_RANK, MLA_HEADS * V_DIM)
    return wk.astype(MXU_DTYPE), wv.astype(MXU_DTYPE)


def _rope_tables(n_lat, n_ctx):
    n_rows = n_lat // GRID_W
    rows = jnp.broadcast_to(jnp.arange(n_rows, dtype=F32)[:, None], (n_rows, GRID_W)).reshape(-1)
    cols = jnp.broadcast_to(jnp.arange(GRID_W, dtype=F32)[None, :], (n_rows, GRID_W)).reshape(-1)
    n_freq = QK_ROPE // 4
    inv = ROPE_BASE ** (-jnp.arange(n_freq, dtype=F32) / n_freq)
    ang = jnp.concatenate([rows[:, None] * inv, cols[:, None] * inv], axis=-1)
    cos, sin = jnp.cos(ang), jnp.sin(ang)
    ones = jnp.ones((n_lat, LANES - QK_ROPE), F32)
    cos_t = jnp.concatenate([cos, cos, ones], axis=1)
    sin_t = jnp.concatenate([sin, sin, 0.0 * ones], axis=1)
    cos_t = jnp.concatenate([cos_t, jnp.ones((n_ctx, LANES), F32)], axis=0)
    sin_t = jnp.concatenate([sin_t, jnp.zeros((n_ctx, LANES), F32)], axis=0)
    return cos_t, sin_t


def _expand_matrix():
    head = _head_perm()[:, None]
    col_head = (jnp.arange(SSM_INNER) // SSM_HEADDIM)[None, :]
    e = jnp.where(head == col_head, 1.0, 0.0).astype(MXU_DTYPE)
    return jnp.concatenate([e, e], axis=0)


def kernel(x, c, ctx, c_ctx, w_ada, b_ada, g_pre, w_in, g_q, w_uq, g_kv, w_ukv, conv_w, conv_b,
           dt_bias_f, dt_bias_b, a_log_f, a_log_b, d_skip, g_ssm, w_proj_a, w_proj_b, w_out, g_final):
    bsz, n_lat, d = x.shape
    n_ctx = ctx.shape[1]
    depth = w_in.shape[0]
    rows = n_lat + n_ctx
    assert d == D_MODEL and n_lat % ATT_TQ == 0 and n_ctx % ATT_TQ == 0 and n_lat % GRID_W == 0
    assert bsz + 1 <= SUBLANES

    hs = jnp.concatenate([x, ctx], axis=1)
    cc = jnp.zeros((SUBLANES, d), F32).at[:bsz].set(c).at[bsz].set(c_ctx)
    cos_t, sin_t = _rope_tables(n_lat, n_ctx)
    e2 = _expand_matrix()
    perm = _head_perm()
    tm_all = _row_tile(rows, 1056)

    for i in range(depth):
        last = i == depth - 1
        out_rows = n_lat if last else rows
        tm_out = _row_tile(out_rows, 1056)
        tm_half = _row_tile(out_rows, 528)

        w_in_r = _prep_w_in(w_in[i])
        wqa, wqb = _prep_w_uq(w_uq[i])
        wk, wv = _prep_w_ukv(w_ukv[i])
        wa = w_proj_a[i].astype(MXU_DTYPE)
        wb = w_proj_b[i].astype(MXU_DTYPE)
        wo = w_out[i].astype(MXU_DTYPE)
        conv_w8 = jnp.zeros((SUBLANES, CONV_CH), F32).at[:CONV_K].set(conv_w[i])

        mod = _ada(cc, w_ada[i], b_ada[i])
        mod3 = mod.reshape(SUBLANES, 1, 3 * d)
        u = _norm_mod(hs, g_pre[i], mod3, n_lat)
        p = _matmul(u, w_in_r, rows=rows, tm=tm_all, tn=1024, out_dtype=F32, name="in_proj")

        q = _q_proj(p, g_q[i], wqa, wqb, cos_t, sin_t, tm_all)
        k, v = _kv_proj(p, g_kv[i], wk, wv, cos_t, sin_t, tm_all)
        oa = _attention(q, k, v, p, n_lat=n_lat, q_rows=out_rows)
        bra = _matmul(oa, wa, rows=out_rows, tm=tm_out, tn=1024, out_dtype=F32, name="proj_a")

        xbc = _conv(p, conv_w8, conv_b[i], n_lat)
        yf = _ssd(xbc, p, dt_bias_f[i][perm], a_log_f[i][perm], e2, n_lat, reverse=False)
        yn = _ssd(xbc, p, dt_bias_b[i][perm], a_log_b[i][perm], e2, n_lat, reverse=True,
                  yf=yf, dskip_x=jnp.repeat(d_skip[i], SSM_HEADDIM), g_ssm=g_ssm[i])

        tn_m = 512
        tile = lambda blk0: pl.BlockSpec((None, tm_half, tn_m), lambda b, r, j: (b, r, blk0 + j))
        merged = _matmul(
            yn, wb, rows=out_rows, tm=tm_half, tn=tn_m, out_dtype=MXU_DTYPE,
            epilogue=_merge_epilogue,
            extras=[(bra, tile(0)), (p, tile(OFF_MGA // tn_m)), (p, tile(OFF_MGB // tn_m))],
            name="proj_b_merge")
        gate_spec = lambda row: pl.BlockSpec((None, 1, tn_m), lambda b, r, j: (row(b), 0, 2 * (d // tn_m) + j))
        hs = _matmul(
            merged, wo, rows=out_rows, tm=tm_out, tn=tn_m, out_dtype=F32,
            epilogue=functools.partial(_residual_epilogue, tm=tm_out, n_lat=n_lat),
            extras=[(hs, pl.BlockSpec((None, tm_out, tn_m), lambda b, r, j: (b, r, j))),
                    (mod3, gate_spec(lambda b: b)), (mod3, gate_spec(lambda b: bsz))],
            name="out_proj")

    return _final_norm(hs, g_final, n_lat)
```

```python
import functools
import math

import jax
import jax.numpy as jnp
from jax import lax
from jax.experimental import pallas as pl
from jax.experimental.pallas import tpu as pltpu

F32 = jnp.float32
MXU_DTYPE = jnp.bfloat16

D_MODEL = 4096
GRID_W = 64
MLA_HEADS = 32
QK_NOPE = 128
QK_ROPE = 64
V_DIM = 128
Q_RANK = 768
KV_RANK = 512
MLA_WIDTH = MLA_HEADS * V_DIM
QK_DIM = QK_NOPE + QK_ROPE
ATTN_SCALE = 1.0 / math.sqrt(QK_DIM)
Q_SCALE = ATTN_SCALE * math.log2(math.e)
ROPE_BASE = 10000.0
SSM_INNER = 2 * D_MODEL
SSM_HEADDIM = 64
SSM_HEADS = SSM_INNER // SSM_HEADDIM
SSM_GROUPS = 8
SSM_HPG = SSM_HEADS // SSM_GROUPS
SSM_STATE = 128
CONV_K = 5
CONV_CH = SSM_INNER + 2 * SSM_GROUPS * SSM_STATE
CHUNK = 128
EPS = 1e-6

LANES = 128
SUBLANES = 8
VMEM_LIMIT_BYTES = 56 * 1024 * 1024

QK_PAD = 2 * LANES
V_PAD = 2 * LANES
GROUP_W = SSM_HPG * SSM_HEADDIM
PAIR_W = 2 * SSM_HEADDIM

OFF_Z = 0
OFF_X = OFF_Z + SSM_INNER
OFF_B = OFF_X + SSM_INNER
OFF_C = OFF_B + SSM_GROUPS * SSM_STATE
OFF_GA = OFF_X + CONV_CH
OFF_MGA = OFF_GA + MLA_WIDTH
OFF_MGB = OFF_MGA + D_MODEL
OFF_CKV = OFF_MGB + D_MODEL
OFF_DT = OFF_CKV + KV_RANK
OFF_CQ = OFF_DT + 2 * SSM_HEADS
OFF_KPE = OFF_CQ + Q_RANK
OFF_KPESW = OFF_KPE + LANES
IN_PAD = 32768
assert OFF_KPESW + LANES <= IN_PAD
assert OFF_CQ % Q_RANK == 0 and OFF_CKV % KV_RANK == 0 and OFF_DT % (2 * SSM_HEADS) == 0


def _sigmoid(x):
    return 1.0 / (1.0 + jnp.exp(-x))


def _silu(x):
    return x * _sigmoid(x)


def _softplus(x):
    return jnp.maximum(x, 0.0) + jnp.log1p(jnp.exp(-jnp.abs(x)))


def _dot(a, b):
    return jnp.dot(a, b, preferred_element_type=F32)


def _dot_nt(a, b):
    return lax.dot_general(a, b, (((1,), (1,)), ((), ())), preferred_element_type=F32)


def _split_hi_lo(x):
    hi = x.astype(MXU_DTYPE)
    lo = (x - hi.astype(F32)).astype(MXU_DTYPE)
    return hi, lo


def _row_tile(rows, target):
    best = None
    for t in range(16, min(rows, target) + 1, 16):
        if rows % t == 0:
            best = t
    assert best is not None, rows
    return best


def _params(*sem):
    return pltpu.CompilerParams(dimension_semantics=sem, vmem_limit_bytes=VMEM_LIMIT_BYTES)


def _ada_kernel(c_ref, w_ref, b_ref, o_ref):
    s = _silu(c_ref[...]).astype(MXU_DTYPE)
    o_ref[...] = _dot(s, w_ref[...].astype(MXU_DTYPE)) + b_ref[...]


def _ada(cc, w, b):
    rows, d = cc.shape
    n = w.shape[1]
    tn = 512
    return pl.pallas_call(
        _ada_kernel,
        grid=(n // tn,),
        in_specs=[pl.BlockSpec((rows, d), lambda j: (0, 0)),
                  pl.BlockSpec((d, tn), lambda j: (0, j)),
                  pl.BlockSpec((1, tn), lambda j: (0, j))],
        out_specs=pl.BlockSpec((rows, tn), lambda j: (0, j)),
        out_shape=jax.ShapeDtypeStruct((rows, n), F32),
        compiler_params=_params("arbitrary"),
        name="ada_mod",
    )(cc, w, b.reshape(1, n))


def _norm_mod_kernel(h_ref, g_ref, sh_ref, sc_ref, o_ref):
    x = h_ref[...]
    ms = jnp.mean(x * x, axis=-1, keepdims=True)
    y = x * lax.rsqrt(ms + EPS) * g_ref[...]
    o_ref[...] = (y * (1.0 + sc_ref[...]) + sh_ref[...]).astype(o_ref.dtype)


def _norm_mod(hs, g, mod3, n_lat):
    bsz, rows, d = hs.shape
    tr = 256
    lat_tiles = n_lat // tr
    ctx_row = bsz

    def mod_row(b, i):
        return jnp.where(i < lat_tiles, b, ctx_row)

    return pl.pallas_call(
        _norm_mod_kernel,
        grid=(bsz, rows // tr),
        in_specs=[pl.BlockSpec((None, tr, d), lambda b, i: (b, i, 0)),
                  pl.BlockSpec((1, d), lambda b, i: (0, 0)),
                  pl.BlockSpec((None, 1, d), lambda b, i: (mod_row(b, i), 0, 0)),
                  pl.BlockSpec((None, 1, d), lambda b, i: (mod_row(b, i), 0, 1))],
        out_specs=pl.BlockSpec((None, tr, d), lambda b, i: (b, i, 0)),
        out_shape=jax.ShapeDtypeStruct((bsz, rows, d), MXU_DTYPE),
        compiler_params=_params("parallel", "arbitrary"),
        name="norm_mod",
    )(hs, g.reshape(1, d), mod3, mod3)


def _final_norm_kernel(h_ref, g_ref, o_ref):
    x = h_ref[...]
    ms = jnp.mean(x * x, axis=-1, keepdims=True)
    o_ref[...] = x * lax.rsqrt(ms + EPS) * g_ref[...]


def _final_norm(hs, g, n_lat):
    bsz, _, d = hs.shape
    tr = 256
    return pl.pallas_call(
        _final_norm_kernel,
        grid=(bsz, n_lat // tr),
        in_specs=[pl.BlockSpec((None, tr, d), lambda b, i: (b, i, 0)),
                  pl.BlockSpec((1, d), lambda b, i: (0, 0))],
        out_specs=pl.BlockSpec((None, tr, d), lambda b, i: (b, i, 0)),
        out_shape=jax.ShapeDtypeStruct((bsz, n_lat, d), F32),
        compiler_params=_params("parallel", "arbitrary"),
        name="final_norm",
    )(hs, g.reshape(1, d))


def _mm_kernel(*refs, n_extra, epilogue):
    x_ref, w_ref = refs[:2]
    extras = refs[2:2 + n_extra]
    o_ref = refs[2 + n_extra]
    acc = _dot(x_ref[...], w_ref[...])
    if epilogue is not None:
        acc = epilogue(acc, *extras)
    o_ref[...] = acc.astype(o_ref.dtype)


def _matmul(x3, w, *, rows, tm, tn, out_dtype, epilogue=None, extras=(), name):
    bsz, r_all, k = x3.shape
    n = w.shape[1]
    assert rows % tm == 0 and n % tn == 0
    in_specs = [pl.BlockSpec((None, tm, k), lambda b, i, j: (b, i, 0)),
                pl.BlockSpec((k, tn), lambda b, i, j: (0, j))]
    in_specs += [spec for _, spec in extras]
    return pl.pallas_call(
        functools.partial(_mm_kernel, n_extra=len(extras), epilogue=epilogue),
        grid=(bsz, rows // tm, n // tn),
        in_specs=in_specs,
        out_specs=pl.BlockSpec((None, tm, tn), lambda b, i, j: (b, i, j)),
        out_shape=jax.ShapeDtypeStruct((bsz, r_all, n), out_dtype),
        compiler_params=_params("parallel", "parallel", "arbitrary"),
        name=name,
    )(x3, w, *[a for a, _ in extras])


def _merge_epilogue(acc_b, bra_ref, mga_ref, mgb_ref):
    return _sigmoid(mga_ref[...]) * bra_ref[...] + _sigmoid(mgb_ref[...]) * acc_b


def _residual_epilogue(acc, h_ref, gate_ref, gate_c_ref, *, tm, n_lat):
    row = pl.program_id(1) * tm + lax.broadcasted_iota(jnp.int32, (tm, 1), 0)
    gate = jnp.where(row < n_lat, gate_ref[...], gate_c_ref[...])
    return h_ref[...] + gate * acc


Q_HEADS_PER_TILE = 4


def _q_kernel(cq_ref, g_ref, wa_ref, wb_ref, cos_ref, sin_ref, o_ref, xn_scr):
    @pl.when(pl.program_id(2) == 0)
    def _():
        x = cq_ref[...]
        ms = jnp.mean(x * x, axis=-1, keepdims=True)
        xn_scr[...] = (x * lax.rsqrt(ms + EPS) * g_ref[...]).astype(xn_scr.dtype)

    xn = xn_scr[...]
    a = _dot(xn, wa_ref[...])
    sw = _dot(xn, wb_ref[...])
    cos = cos_ref[...]
    sin = sin_ref[...]
    for hh in range(Q_HEADS_PER_TILE):
        base = hh * QK_PAD
        o_ref[:, base:base + LANES] = (a[:, base:base + LANES] * Q_SCALE).astype(o_ref.dtype)
        rope = a[:, base + LANES:base + QK_PAD] * cos + sw[:, hh * LANES:(hh + 1) * LANES] * sin
        o_ref[:, base + LANES:base + QK_PAD] = (rope * Q_SCALE).astype(o_ref.dtype)


def _q_proj(p, g_q, wqa, wqb, cos_t, sin_t, tm):
    bsz, rows, _ = p.shape
    tn = Q_HEADS_PER_TILE * QK_PAD
    n = MLA_HEADS * QK_PAD
    return pl.pallas_call(
        _q_kernel,
        grid=(bsz, rows // tm, n // tn),
        in_specs=[pl.BlockSpec((None, tm, Q_RANK), lambda b, i, j: (b, i, OFF_CQ // Q_RANK)),
                  pl.BlockSpec((1, Q_RANK), lambda b, i, j: (0, 0)),
                  pl.BlockSpec((Q_RANK, tn), lambda b, i, j: (0, j)),
                  pl.BlockSpec((Q_RANK, tn // 2), lambda b, i, j: (0, j)),
                  pl.BlockSpec((tm, LANES), lambda b, i, j: (i, 0)),
                  pl.BlockSpec((tm, LANES), lambda b, i, j: (i, 0))],
        out_specs=pl.BlockSpec((None, tm, tn), lambda b, i, j: (b, i, j)),
        out_shape=jax.ShapeDtypeStruct((bsz, rows, n), MXU_DTYPE),
        scratch_shapes=[pltpu.VMEM((tm, Q_RANK), MXU_DTYPE)],
        compiler_params=_params("parallel", "parallel", "arbitrary"),
        name="q_up",
    )(p, g_q.reshape(1, Q_RANK), wqa, wqb, cos_t, sin_t)


def _kv_kernel(ckv_ref, g_ref, wk_ref, wv_ref, kpe_ref, kpesw_ref, cos_ref, sin_ref,
               k_ref, v_ref, xn_scr, kp_scr):
    @pl.when(pl.program_id(2) == 0)
    def _():
        x = ckv_ref[...]
        ms = jnp.mean(x * x, axis=-1, keepdims=True)
        xn_scr[...] = (x * lax.rsqrt(ms + EPS) * g_ref[...]).astype(xn_scr.dtype)
        kp_scr[...] = (kpe_ref[...] * cos_ref[...] + kpesw_ref[...] * sin_ref[...]).astype(kp_scr.dtype)

    xn = xn_scr[...]
    kn = _dot(xn, wk_ref[...])
    vv = _dot(xn, wv_ref[...])
    lane = lax.broadcasted_iota(jnp.int32, (xn.shape[0], LANES), 1)
    one_col = jnp.where(lane == 0, 1.0, 0.0).astype(v_ref.dtype)
    for hh in range(Q_HEADS_PER_TILE):
        base = hh * QK_PAD
        k_ref[:, base:base + LANES] = kn[:, hh * LANES:(hh + 1) * LANES].astype(k_ref.dtype)
        k_ref[:, base + LANES:base + QK_PAD] = kp_scr[...]
        v_ref[:, base:base + LANES] = vv[:, hh * LANES:(hh + 1) * LANES].astype(v_ref.dtype)
        v_ref[:, base + LANES:base + V_PAD] = one_col


def _kv_proj(p, g_kv, wk, wv, cos_t, sin_t, tm):
    bsz, rows, _ = p.shape
    tk = Q_HEADS_PER_TILE * QK_PAD
    tw = Q_HEADS_PER_TILE * V_DIM
    tv = Q_HEADS_PER_TILE * V_PAD
    nk = MLA_HEADS * QK_PAD
    nv = MLA_HEADS * V_PAD
    return pl.pallas_call(
        _kv_kernel,
        grid=(bsz, rows // tm, nk // tk),
        in_specs=[pl.BlockSpec((None, tm, KV_RANK), lambda b, i, j: (b, i, OFF_CKV // KV_RANK)),
                  pl.BlockSpec((1, KV_RANK), lambda b, i, j: (0, 0)),
                  pl.BlockSpec((KV_RANK, tw), lambda b, i, j: (0, j)),
                  pl.BlockSpec((KV_RANK, tw), lambda b, i, j: (0, j)),
                  pl.BlockSpec((None, tm, LANES), lambda b, i, j: (b, i, OFF_KPE // LANES)),
                  pl.BlockSpec((None, tm, LANES), lambda b, i, j: (b, i, OFF_KPESW // LANES)),
                  pl.BlockSpec((tm, LANES), lambda b, i, j: (i, 0)),
                  pl.BlockSpec((tm, LANES), lambda b, i, j: (i, 0))],
        out_specs=[pl.BlockSpec((None, tm, tk), lambda b, i, j: (b, i, j)),
                   pl.BlockSpec((None, tm, tv), lambda b, i, j: (b, i, j))],
        out_shape=[jax.ShapeDtypeStruct((bsz, rows, nk), MXU_DTYPE),
                   jax.ShapeDtypeStruct((bsz, rows, nv), MXU_DTYPE)],
        scratch_shapes=[pltpu.VMEM((tm, KV_RANK), MXU_DTYPE), pltpu.VMEM((tm, LANES), MXU_DTYPE)],
        compiler_params=_params("parallel", "parallel", "arbitrary"),
        name="kv_up",
    )(p, g_kv.reshape(1, KV_RANK), wk, wv, p, p, cos_t, sin_t)


ATT_TQ = 512
ATT_TQ_CTX = 256
ATT_TK = 1024


def _key_chunk(key_rows):
    best = None
    for t in range(QK_PAD, min(key_rows, ATT_TK) + 1, QK_PAD):
        if key_rows % t == 0:
            best = t
    assert best is not None, key_rows
    return best


def _attn_kernel(q_ref, k_ref, v_ref, ga_ref, o_ref, m_scr, acc_scr, s_a, s_b, *, n_chunks, tk):
    q = q_ref[...]
    m_scr[...] = jnp.full(m_scr.shape, -jnp.inf, F32)
    acc_scr[...] = jnp.zeros(acc_scr.shape, F32)

    def rows_of(c):
        r0 = c * tk
        return r0 if isinstance(r0, int) else pl.multiple_of(r0, tk)

    def scores(c, dst):
        dst[...] = _dot_nt(q, k_ref[pl.ds(rows_of(c), tk), :])

    def absorb(c, src):
        s = src[...]
        m_old = m_scr[...]
        m_new = jnp.maximum(m_old, jnp.max(s, axis=-1, keepdims=True))
        p = jnp.exp2(s - m_new).astype(v_ref.dtype)
        acc_scr[...] = jnp.exp2(m_old - m_new) * acc_scr[...] + _dot(p, v_ref[pl.ds(rows_of(c), tk), :])
        m_scr[...] = m_new

    scores(0, s_a)
    n_pairs = (n_chunks - 1) // 2

    def pair(i, carry):
        c = 2 * i
        scores(c + 1, s_b)
        absorb(c, s_a)
        scores(c + 2, s_a)
        absorb(c + 1, s_b)
        return carry

    if n_pairs:
        lax.fori_loop(0, n_pairs, pair, 0)
    c = 2 * n_pairs
    if n_chunks - c == 2:
        scores(c + 1, s_b)
        absorb(c, s_a)
        absorb(c + 1, s_b)
    else:
        absorb(c, s_a)
    acc = acc_scr[...]
    o = acc[:, :V_DIM] / acc[:, V_DIM:V_DIM + 1]
    o_ref[...] = (o * _silu(ga_ref[...])).astype(o_ref.dtype)


def _attn_kernel_aliased(q_ref, k_ref, v_ref, ga_ref, prev_ref, o_ref, *scratch, **kw):
    del prev_ref
    _attn_kernel(q_ref, k_ref, v_ref, ga_ref, o_ref, *scratch, **kw)


def _attention(q, k, v, p, *, tq, q_row0, q_rows, key_row0, key_rows, prev=None):
    bsz, rows, _ = q.shape
    assert q_row0 % tq == 0 and q_rows % tq == 0 and key_row0 % key_rows == 0
    qb, kb = q_row0 // tq, key_row0 // key_rows
    tk = _key_chunk(key_rows)
    kw = dict(n_chunks=key_rows // tk, tk=tk)
    in_specs = [pl.BlockSpec((None, tq, QK_PAD), lambda b, h, i: (b, qb + i, h)),
                pl.BlockSpec((None, key_rows, QK_PAD), lambda b, h, i: (b, kb, h)),
                pl.BlockSpec((None, key_rows, V_PAD), lambda b, h, i: (b, kb, h)),
                pl.BlockSpec((None, tq, V_DIM), lambda b, h, i: (b, qb + i, OFF_GA // V_DIM + h))]
    args = [q, k, v, p]
    if prev is None:
        kern, aliases, name = functools.partial(_attn_kernel, **kw), {}, "attention"
    else:
        kern, aliases, name = functools.partial(_attn_kernel_aliased, **kw), {4: 0}, "attention_ctx"
        in_specs.append(pl.BlockSpec(memory_space=pl.ANY))
        args.append(prev)
    return pl.pallas_call(
        kern,
        grid=(bsz, MLA_HEADS, q_rows // tq),
        in_specs=in_specs,
        out_specs=pl.BlockSpec((None, tq, V_DIM), lambda b, h, i: (b, qb + i, h)),
        out_shape=jax.ShapeDtypeStruct((bsz, rows, MLA_WIDTH), MXU_DTYPE),
        scratch_shapes=[pltpu.VMEM((tq, 1), F32), pltpu.VMEM((tq, V_PAD), F32),
                        pltpu.VMEM((tq, tk), F32), pltpu.VMEM((tq, tk), F32)],
        input_output_aliases=aliases,
        compiler_params=_params("parallel", "parallel", "arbitrary"),
        name=name,
    )(*args)


CONV_TC = 2048
CONV_TR = 2 * CHUNK
HALO = SUBLANES


def _conv_kernel(prev_ref, cur_ref, next_ref, w_ref, b_ref, o_ref, ext, *, lat_tiles, all_tiles):
    c = pl.program_id(1)
    has_left = jnp.logical_and(c != 0, c != lat_tiles)
    has_right = jnp.logical_and(c != lat_tiles - 1, c != all_tiles - 1)
    ext[0:HALO, :] = jnp.where(has_left, prev_ref[...], 0.0)
    ext[HALO:HALO + CONV_TR, :] = cur_ref[...]
    ext[HALO + CONV_TR:, :] = jnp.where(has_right, next_ref[...], 0.0)
    pad = (CONV_K - 1) // 2
    acc = jnp.broadcast_to(b_ref[...], (CONV_TR, CONV_TC))
    for kk in range(CONV_K):
        lo = HALO - pad + kk
        acc = acc + w_ref[kk:kk + 1, :] * ext[lo:lo + CONV_TR, :]
    o_ref[...] = _silu(acc)


def _conv(p, w8, bias, n_lat):
    bsz, rows, _ = p.shape
    all_tiles = rows // CONV_TR
    lat_tiles = n_lat // CONV_TR
    per = CONV_TR // HALO
    cb = OFF_X // CONV_TC
    last_halo = rows // HALO - 1
    kern = functools.partial(_conv_kernel, lat_tiles=lat_tiles, all_tiles=all_tiles)
    return pl.pallas_call(
        kern,
        grid=(bsz, all_tiles, CONV_CH // CONV_TC),
        in_specs=[pl.BlockSpec((None, HALO, CONV_TC),
                               lambda b, c, j: (b, jnp.maximum(c * per - 1, 0), cb + j)),
                  pl.BlockSpec((None, CONV_TR, CONV_TC), lambda b, c, j: (b, c, cb + j)),
                  pl.BlockSpec((None, HALO, CONV_TC),
                               lambda b, c, j: (b, jnp.minimum((c + 1) * per, last_halo), cb + j)),
                  pl.BlockSpec((SUBLANES, CONV_TC), lambda b, c, j: (0, j)),
                  pl.BlockSpec((1, CONV_TC), lambda b, c, j: (0, j))],
        out_specs=pl.BlockSpec((None, CONV_TR, CONV_TC), lambda b, c, j: (b, c, j)),
        out_shape=jax.ShapeDtypeStruct((bsz, rows, CONV_CH), F32),
        scratch_shapes=[pltpu.VMEM((CONV_TR + 2 * HALO, CONV_TC), F32)],
        compiler_params=_params("parallel", "parallel", "arbitrary"),
        name="dwconv",
    )(p, p, p, w8, bias.reshape(1, CONV_CH))


def _ssd_kernel(*refs, reverse):
    if reverse:
        (x_ref, b_ref, c_ref, dt_ref, bias_ref, alog_ref, e2_ref,
         yf_ref, z_ref, dsk_ref, gs_ref, o_ref, state, rrow, ybuf) = refs
        dest = ybuf
    else:
        (x_ref, b_ref, c_ref, dt_ref, bias_ref, alog_ref, e2_ref, o_ref, state, rrow) = refs
        dest = o_ref
    L = CHUNK
    half = PAIR_W // 2

    @pl.when(pl.program_id(1) == 0)
    def _():
        state[...] = jnp.zeros_like(state)

    off = SSM_HEADS if reverse else 0
    dt = _softplus(dt_ref[:, off:off + SSM_HEADS] + bias_ref[...])
    a = dt * (-jnp.exp(alog_ref[...]))

    ri = lax.broadcasted_iota(jnp.int32, (L, L), 0)
    ci = lax.broadcasted_iota(jnp.int32, (L, L), 1)
    tri = jnp.where((ri <= ci) if reverse else (ri >= ci), 1.0, 0.0).astype(MXU_DTYPE)
    p1 = a.astype(MXU_DTYPE)
    r1 = a - p1.astype(F32)
    p2 = r1.astype(MXU_DTYPE)
    p3 = (r1 - p2.astype(F32)).astype(MXU_DTYPE)
    acs = _dot(tri, p1) + _dot(tri, p2) + _dot(tri, p3)

    acs_t = acs.T
    lane_h = lax.broadcasted_iota(jnp.int32, (SSM_HEADS // 2, LANES), 1)
    ev = acs_t[0:SSM_HEADS // 2]
    od = acs_t[SSM_HEADS // 2:]
    rrow[0] = jnp.where(lane_h < half, ev, pltpu.roll(od, half, 1))
    rrow[1] = jnp.where(lane_h < half, pltpu.roll(ev, half, 1), od)

    acs_hl = jnp.concatenate(_split_hi_lo(acs), axis=1)
    dt_hl = jnp.concatenate(_split_hi_lo(dt), axis=1)
    edge = 0 if reverse else L - 1

    li = lax.broadcasted_iota(jnp.int32, (L, LANES), 0)
    lane = lax.broadcasted_iota(jnp.int32, (L, LANES), 1)
    src = jnp.where(lane < half, lane, lane - half)
    masks = []
    for sb in range(2):
        s_pos = src + sb * half
        masks.append((li <= s_pos) if reverse else (li >= s_pos))
    lane_s = lax.broadcasted_iota(jnp.int32, (half, LANES), 1)
    first_head = lane_s < half

    def group(g, carry):
        c0 = pl.multiple_of(g * GROUP_W, GROUP_W)
        n0 = pl.multiple_of(g * SSM_STATE, SSM_STATE)
        e2g = e2_ref[:, pl.ds(c0, GROUP_W)]
        acs_x = _dot(acs_hl, e2g)
        dt_x = _dot(dt_hl, e2g)
        xdt = x_ref[:, pl.ds(c0, GROUP_W)] * dt_x
        bg = b_ref[:, pl.ds(n0, SSM_STATE)]
        cg = c_ref[:, pl.ds(n0, SSM_STATE)].astype(MXU_DTYPE)
        bgb = bg.astype(MXU_DTYPE)
        last_x = acs_x[edge:edge + 1, :]

        sg = state[g]
        y_off = _dot(cg, sg.astype(MXU_DTYPE)) * jnp.exp(acs_x)
        wgt = (xdt * jnp.exp(last_x - acs_x)).astype(MXU_DTYPE)
        state[g] = sg * jnp.exp(last_x) + _dot(bg.T.astype(MXU_DTYPE), wgt)

        cbd = []
        for sb in range(2):
            bs = bgb[sb * half:(sb + 1) * half]
            cbd.append(_dot_nt(cg, jnp.concatenate([bs, bs], axis=0)))

        for jj in range(GROUP_W // PAIR_W):
            col = acs_x[:, jj * PAIR_W:(jj + 1) * PAIR_W]
            yp = y_off[:, jj * PAIR_W:(jj + 1) * PAIR_W]
            for sb in range(2):
                row = rrow[sb, pl.ds(g * (GROUP_W // PAIR_W) + jj, 1), :]
                lm = jnp.exp(jnp.where(masks[sb], col - row, -jnp.inf))
                mp = (cbd[sb] * lm).astype(MXU_DTYPE)
                xs = xdt[sb * half:(sb + 1) * half, jj * PAIR_W:(jj + 1) * PAIR_W]
                xbd = jnp.concatenate([jnp.where(first_head, xs, 0.0),
                                       jnp.where(first_head, 0.0, xs)], axis=0).astype(MXU_DTYPE)
                yp = yp + _dot(mp, xbd)
            dest[:, pl.ds(pl.multiple_of(c0 + jj * PAIR_W, PAIR_W), PAIR_W)] = yp
        return carry

    lax.fori_loop(0, SSM_GROUPS, group, 0)

    if reverse:
        ssq = jnp.zeros((L, 1), F32)
        for g in range(SSM_GROUPS):
            sl = slice(g * GROUP_W, (g + 1) * GROUP_W)
            y = ybuf[:, sl] + yf_ref[:, sl] + x_ref[:, sl] * dsk_ref[:, sl]
            yz = y * _silu(z_ref[:, sl])
            ybuf[:, sl] = yz
            ssq = ssq + jnp.sum(yz * yz, axis=-1, keepdims=True)
        r = lax.rsqrt(ssq / SSM_INNER + EPS)
        for g in range(SSM_GROUPS):
            sl = slice(g * GROUP_W, (g + 1) * GROUP_W)
            o_ref[:, sl] = (ybuf[:, sl] * r * gs_ref[:, sl]).astype(o_ref.dtype)


def _ssd(xbc, p, bias, alog, e2, n_lat, *, reverse, yf=None, dskip_x=None, g_ssm=None):
    bsz, rows, _ = xbc.shape
    all_chunks = rows // CHUNK
    lat_chunks = n_lat // CHUNK
    ctx_chunks = all_chunks - lat_chunks

    if reverse:
        def chunk(s):
            return all_chunks - 1 - s
    else:
        def chunk(s):
            return jnp.where(s < ctx_chunks, lat_chunks + s, s - ctx_chunks)

    row_spec = lambda width, blk: pl.BlockSpec((None, CHUNK, width), lambda b, s: (b, chunk(s), blk))
    const = lambda shape: pl.BlockSpec(shape, lambda b, s: (0,) * len(shape))
    ng = SSM_GROUPS * SSM_STATE
    in_specs = [row_spec(SSM_INNER, 0),
                row_spec(ng, SSM_INNER // ng),
                row_spec(ng, SSM_INNER // ng + 1),
                row_spec(2 * SSM_HEADS, OFF_DT // (2 * SSM_HEADS)),
                const((1, SSM_HEADS)), const((1, SSM_HEADS)), const((2 * SSM_HEADS, SSM_INNER))]
    args = [xbc, xbc, xbc, p, bias.reshape(1, SSM_HEADS), alog.reshape(1, SSM_HEADS), e2]
    scratch = [pltpu.VMEM((SSM_GROUPS, SSM_STATE, GROUP_W), F32),
               pltpu.VMEM((2, SSM_HEADS // 2, LANES), F32)]
    if reverse:
        in_specs += [row_spec(SSM_INNER, 0), row_spec(SSM_INNER, OFF_Z // SSM_INNER),
                     const((1, SSM_INNER)), const((1, SSM_INNER))]
        args += [yf, p, dskip_x.reshape(1, SSM_INNER), g_ssm.reshape(1, SSM_INNER)]
        scratch.append(pltpu.VMEM((CHUNK, SSM_INNER), F32))
        out_dtype = MXU_DTYPE
    else:
        out_dtype = F32
    return pl.pallas_call(
        functools.partial(_ssd_kernel, reverse=reverse),
        grid=(bsz, all_chunks),
        in_specs=in_specs,
        out_specs=row_spec(SSM_INNER, 0),
        out_shape=jax.ShapeDtypeStruct((bsz, rows, SSM_INNER), out_dtype),
        scratch_shapes=scratch,
        compiler_params=_params("parallel", "arbitrary"),
        name="ssd_bwd" if reverse else "ssd_fwd",
    )(*args)


def _head_perm():
    k = jnp.arange(SSM_HEADS)
    return jnp.where(k < SSM_HEADS // 2, 2 * k, 2 * (k - SSM_HEADS // 2) + 1)


def _prep_w_in(w):
    sizes = (Q_RANK, KV_RANK, QK_ROPE, MLA_WIDTH, SSM_INNER, SSM_INNER,
             SSM_GROUPS * SSM_STATE, SSM_GROUPS * SSM_STATE, SSM_HEADS, SSM_HEADS, D_MODEL, D_MODEL)
    offs = [0]
    for s in sizes:
        offs.append(offs[-1] + s)
    (cq, ckv, kpe, ga, z, xs, bs, cs, dtf, dtb, mga, mgb) = [w[:, offs[i]:offs[i + 1]] for i in range(12)]
    perm = _head_perm()
    zpad = jnp.zeros((w.shape[0], LANES - QK_ROPE), w.dtype)
    k1, k2 = kpe[:, :QK_ROPE // 2], kpe[:, QK_ROPE // 2:]
    cols = [z, xs, bs, cs, ga, mga, mgb, ckv, dtf[:, perm], dtb[:, perm], cq,
            kpe, zpad, -k2, k1, zpad]
    used = sum(c.shape[1] for c in cols)
    cols.append(jnp.zeros((w.shape[0], IN_PAD - used), w.dtype))
    return jnp.concatenate(cols, axis=1).astype(MXU_DTYPE)


def _prep_w_uq(w):
    w3 = w.reshape(Q_RANK, MLA_HEADS, QK_DIM)
    nope = w3[..., :QK_NOPE]
    r1 = w3[..., QK_NOPE:QK_NOPE + QK_ROPE // 2]
    r2 = w3[..., QK_NOPE + QK_ROPE // 2:]
    zp = jnp.zeros((Q_RANK, MLA_HEADS, LANES - QK_ROPE), w.dtype)
    wa = jnp.concatenate([nope, r1, r2, zp], axis=-1).reshape(Q_RANK, MLA_HEADS * QK_PAD)
    wb = jnp.concatenate([-r2, r1, zp], axis=-1).reshape(Q_RANK, MLA_HEADS * LANES)
    return wa.astype(MXU_DTYPE), wb.astype(MXU_DTYPE)


def _prep_w_ukv(w):
    w3 = w.reshape(KV_RANK, MLA_HEADS, QK_NOPE + V_DIM)
    wk = w3[..., :QK_NOPE].reshape(KV_RANK, MLA_HEADS * QK_NOPE)
    wv = w3[..., QK_NOPE:].reshape(KV_RANK, MLA_HEADS * V_DIM)
    return wk.astype(MXU_DTYPE), wv.astype(MXU_DTYPE)


def _rope_tables(n_lat, n_ctx):
    n_rows = n_lat // GRID_W
    rows = jnp.broadcast_to(jnp.arange(n_rows, dtype=F32)[:, None], (n_rows, GRID_W)).reshape(-1)
    cols = jnp.broadcast_to(jnp.arange(GRID_W, dtype=F32)[None, :], (n_rows, GRID_W)).reshape(-1)
    n_freq = QK_ROPE // 4
    inv = ROPE_BASE ** (-jnp.arange(n_freq, dtype=F32) / n_freq)
    ang = jnp.concatenate([rows[:, None] * inv, cols[:, None] * inv], axis=-1)
    cos, sin = jnp.cos(ang), jnp.sin(ang)
    ones = jnp.ones((n_lat, LANES - QK_ROPE), F32)
    cos_t = jnp.concatenate([cos, cos, ones], axis=1)
    sin_t = jnp.concatenate([sin, sin, 0.0 * ones], axis=1)
    cos_t = jnp.concatenate([cos_t, jnp.ones((n_ctx, LANES), F32)], axis=0)
    sin_t = jnp.concatenate([sin_t, jnp.zeros((n_ctx, LANES), F32)], axis=0)
    return cos_t, sin_t


def _expand_matrix():
    head = _head_perm()[:, None]
    col_head = (jnp.arange(SSM_INNER) // SSM_HEADDIM)[None, :]
    e = jnp.where(head == col_head, 1.0, 0.0).astype(MXU_DTYPE)
    return jnp.concatenate([e, e], axis=0)


def kernel(x, c, ctx, c_ctx, w_ada, b_ada, g_pre, w_in, g_q, w_uq, g_kv, w_ukv, conv_w, conv_b,
           dt_bias_f, dt_bias_b, a_log_f, a_log_b, d_skip, g_ssm, w_proj_a, w_proj_b, w_out, g_final):
    bsz, n_lat, d = x.shape
    n_ctx = ctx.shape[1]
    depth = w_in.shape[0]
    rows = n_lat + n_ctx
    assert d == D_MODEL and n_lat % GRID_W == 0 and n_lat % n_ctx == 0 and n_ctx % (2 * CHUNK) == 0
    assert bsz + 1 <= SUBLANES

    hs = jnp.concatenate([x, ctx], axis=1)
    cc = jnp.zeros((SUBLANES, d), F32).at[:bsz].set(c).at[bsz].set(c_ctx)
    cos_t, sin_t = _rope_tables(n_lat, n_ctx)
    e2 = _expand_matrix()
    perm = _head_perm()
    tm_all = _row_tile(rows, 1056)

    for i in range(depth):
        last = i == depth - 1
        out_rows = n_lat if last else rows
        tm_out = _row_tile(out_rows, 1056)
        tm_half = _row_tile(out_rows, 528)

        w_in_r = _prep_w_in(w_in[i])
        wqa, wqb = _prep_w_uq(w_uq[i])
        wk, wv = _prep_w_ukv(w_ukv[i])
        wa = w_proj_a[i].astype(MXU_DTYPE)
        wb = w_proj_b[i].astype(MXU_DTYPE)
        wo = w_out[i].astype(MXU_DTYPE)
        conv_w8 = jnp.zeros((SUBLANES, CONV_CH), F32).at[:CONV_K].set(conv_w[i])

        mod = _ada(cc, w_ada[i], b_ada[i])
        mod3 = mod.reshape(SUBLANES, 1, 3 * d)
        u = _norm_mod(hs, g_pre[i], mod3, n_lat)
        p = _matmul(u, w_in_r, rows=rows, tm=tm_all, tn=1024, out_dtype=F32, name="in_proj")

        q = _q_proj(p, g_q[i], wqa, wqb, cos_t, sin_t, tm_all)
        k, v = _kv_proj(p, g_kv[i], wk, wv, cos_t, sin_t, tm_all)
        oa = _attention(q, k, v, p, tq=_row_tile(n_lat, ATT_TQ), q_row0=0, q_rows=n_lat,
                        key_row0=0, key_rows=rows)
        if not last:
            oa = _attention(q, k, v, p, tq=_row_tile(n_ctx, ATT_TQ_CTX), q_row0=n_lat, q_rows=n_ctx,
                            key_row0=n_lat, key_rows=n_ctx, prev=oa)
        bra = _matmul(oa, wa, rows=out_rows, tm=tm_out, tn=1024, out_dtype=F32, name="proj_a")

        xbc = _conv(p, conv_w8, conv_b[i], n_lat)
        yf = _ssd(xbc, p, dt_bias_f[i][perm], a_log_f[i][perm], e2, n_lat, reverse=False)
        yn = _ssd(xbc, p, dt_bias_b[i][perm], a_log_b[i][perm], e2, n_lat, reverse=True,
                  yf=yf, dskip_x=jnp.repeat(d_skip[i], SSM_HEADDIM), g_ssm=g_ssm[i])

        tn_m = 512
        tile = lambda blk0: pl.BlockSpec((None, tm_half, tn_m), lambda b, r, j: (b, r, blk0 + j))
        merged = _matmul(
            yn, wb, rows=out_rows, tm=tm_half, tn=tn_m, out_dtype=MXU_DTYPE,
            epilogue=_merge_epilogue,
            extras=[(bra, tile(0)), (p, tile(OFF_MGA // tn_m)), (p, tile(OFF_MGB // tn_m))],
            name="proj_b_merge")
        gate_spec = lambda row: pl.BlockSpec((None, 1, tn_m), lambda b, r, j: (row(b), 0, 2 * (d // tn_m) + j))
        hs = _matmul(
            merged, wo, rows=out_rows, tm=tm_out, tn=tn_m, out_dtype=F32,
            epilogue=functools.partial(_residual_epilogue, tm=tm_out, n_lat=n_lat),
            extras=[(hs, pl.BlockSpec((None, tm_out, tn_m), lambda b, r, j: (b, r, j))),
                    (mod3, gate_spec(lambda b: b)), (mod3, gate_spec(lambda b: bsz))],
            name="out_proj")

    return _final_norm(hs, g_final, n_lat)
```

```python
import functools
import math

import jax
import jax.numpy as jnp
from jax import lax
from jax.experimental import pallas as pl
from jax.experimental.pallas import tpu as pltpu

F32 = jnp.float32
MXU_DTYPE = jnp.bfloat16

D_MODEL = 4096
GRID_W = 64
MLA_HEADS = 32
QK_NOPE = 128
QK_ROPE = 64
V_DIM = 128
Q_RANK = 768
KV_RANK = 512
MLA_WIDTH = MLA_HEADS * V_DIM
QK_DIM = QK_NOPE + QK_ROPE
ATTN_SCALE = 1.0 / math.sqrt(QK_DIM)
Q_SCALE = ATTN_SCALE * math.log2(math.e)
ROPE_BASE = 10000.0
SSM_INNER = 2 * D_MODEL
SSM_HEADDIM = 64
SSM_HEADS = SSM_INNER // SSM_HEADDIM
SSM_GROUPS = 8
SSM_HPG = SSM_HEADS // SSM_GROUPS
SSM_STATE = 128
CONV_K = 5
CONV_CH = SSM_INNER + 2 * SSM_GROUPS * SSM_STATE
CHUNK = 128
EPS = 1e-6

LANES = 128
SUBLANES = 8
VMEM_LIMIT_BYTES = 56 * 1024 * 1024

QK_PAD = 2 * LANES
VT_ROWS = V_DIM + 16
GROUP_W = SSM_HPG * SSM_HEADDIM
PAIR_W = 2 * SSM_HEADDIM

OFF_Z = 0
OFF_X = OFF_Z + SSM_INNER
OFF_B = OFF_X + SSM_INNER
OFF_C = OFF_B + SSM_GROUPS * SSM_STATE
OFF_GA = OFF_X + CONV_CH
OFF_MGA = OFF_GA + MLA_WIDTH
OFF_MGB = OFF_MGA + D_MODEL
OFF_CKV = OFF_MGB + D_MODEL
OFF_DT = OFF_CKV + KV_RANK
OFF_CQ = OFF_DT + 2 * SSM_HEADS
OFF_KPE = OFF_CQ + Q_RANK
OFF_KPESW = OFF_KPE + LANES
IN_PAD = 32768
assert OFF_KPESW + LANES <= IN_PAD
assert OFF_CQ % Q_RANK == 0 and OFF_CKV % KV_RANK == 0 and OFF_DT % (2 * SSM_HEADS) == 0


def _sigmoid(x):
    return 1.0 / (1.0 + jnp.exp(-x))


def _silu(x):
    return x * _sigmoid(x)


def _softplus(x):
    return jnp.maximum(x, 0.0) + jnp.log1p(jnp.exp(-jnp.abs(x)))


def _dot(a, b):
    return jnp.dot(a, b, preferred_element_type=F32)


def _dot_nt(a, b):
    return lax.dot_general(a, b, (((1,), (1,)), ((), ())), preferred_element_type=F32)


def _split_hi_lo(x):
    hi = x.astype(MXU_DTYPE)
    lo = (x - hi.astype(F32)).astype(MXU_DTYPE)
    return hi, lo


def _row_tile(rows, target):
    best = None
    for t in range(16, min(rows, target) + 1, 16):
        if rows % t == 0:
            best = t
    assert best is not None, rows
    return best


def _lane_tile(rows, target):
    best = None
    for t in range(LANES, min(rows, target) + 1, LANES):
        if rows % t == 0:
            best = t
    assert best is not None, rows
    return best


def _params(*sem):
    return pltpu.CompilerParams(dimension_semantics=sem, vmem_limit_bytes=VMEM_LIMIT_BYTES)


def _ada_kernel(c_ref, w_ref, b_ref, o_ref):
    s = _silu(c_ref[...]).astype(MXU_DTYPE)
    o_ref[...] = _dot(s, w_ref[...].astype(MXU_DTYPE)) + b_ref[...]


def _ada(cc, w, b):
    rows, d = cc.shape
    n = w.shape[1]
    tn = 512
    return pl.pallas_call(
        _ada_kernel,
        grid=(n // tn,),
        in_specs=[pl.BlockSpec((rows, d), lambda j: (0, 0)),
                  pl.BlockSpec((d, tn), lambda j: (0, j)),
                  pl.BlockSpec((1, tn), lambda j: (0, j))],
        out_specs=pl.BlockSpec((rows, tn), lambda j: (0, j)),
        out_shape=jax.ShapeDtypeStruct((rows, n), F32),
        compiler_params=_params("arbitrary"),
        name="ada_mod",
    )(cc, w, b.reshape(1, n))


def _norm_mod_kernel(h_ref, g_ref, sh_ref, sc_ref, o_ref):
    x = h_ref[...]
    ms = jnp.mean(x * x, axis=-1, keepdims=True)
    y = x * lax.rsqrt(ms + EPS) * g_ref[...]
    o_ref[...] = (y * (1.0 + sc_ref[...]) + sh_ref[...]).astype(o_ref.dtype)


def _norm_mod(hs, g, mod3, n_lat):
    bsz, rows, d = hs.shape
    tr = 256
    lat_tiles = n_lat // tr
    ctx_row = bsz

    def mod_row(b, i):
        return jnp.where(i < lat_tiles, b, ctx_row)

    return pl.pallas_call(
        _norm_mod_kernel,
        grid=(bsz, rows // tr),
        in_specs=[pl.BlockSpec((None, tr, d), lambda b, i: (b, i, 0)),
                  pl.BlockSpec((1, d), lambda b, i: (0, 0)),
                  pl.BlockSpec((None, 1, d), lambda b, i: (mod_row(b, i), 0, 0)),
                  pl.BlockSpec((None, 1, d), lambda b, i: (mod_row(b, i), 0, 1))],
        out_specs=pl.BlockSpec((None, tr, d), lambda b, i: (b, i, 0)),
        out_shape=jax.ShapeDtypeStruct((bsz, rows, d), MXU_DTYPE),
        compiler_params=_params("parallel", "arbitrary"),
        name="norm_mod",
    )(hs, g.reshape(1, d), mod3, mod3)


def _final_norm_kernel(h_ref, g_ref, o_ref):
    x = h_ref[...]
    ms = jnp.mean(x * x, axis=-1, keepdims=True)
    o_ref[...] = x * lax.rsqrt(ms + EPS) * g_ref[...]


def _final_norm(hs, g, n_lat):
    bsz, _, d = hs.shape
    tr = 256
    return pl.pallas_call(
        _final_norm_kernel,
        grid=(bsz, n_lat // tr),
        in_specs=[pl.BlockSpec((None, tr, d), lambda b, i: (b, i, 0)),
                  pl.BlockSpec((1, d), lambda b, i: (0, 0))],
        out_specs=pl.BlockSpec((None, tr, d), lambda b, i: (b, i, 0)),
        out_shape=jax.ShapeDtypeStruct((bsz, n_lat, d), F32),
        compiler_params=_params("parallel", "arbitrary"),
        name="final_norm",
    )(hs, g.reshape(1, d))


def _mm_kernel(*refs, n_extra, epilogue):
    x_ref, w_ref = refs[:2]
    extras = refs[2:2 + n_extra]
    o_ref = refs[2 + n_extra]
    acc = _dot(x_ref[...], w_ref[...])
    if epilogue is not None:
        acc = epilogue(acc, *extras)
    o_ref[...] = acc.astype(o_ref.dtype)


def _matmul(x3, w, *, rows, tm, tn, out_dtype, epilogue=None, extras=(), name):
    bsz, r_all, k = x3.shape
    n = w.shape[1]
    assert rows % tm == 0 and n % tn == 0
    in_specs = [pl.BlockSpec((None, tm, k), lambda b, i, j: (b, i, 0)),
                pl.BlockSpec((k, tn), lambda b, i, j: (0, j))]
    in_specs += [spec for _, spec in extras]
    return pl.pallas_call(
        functools.partial(_mm_kernel, n_extra=len(extras), epilogue=epilogue),
        grid=(bsz, rows // tm, n // tn),
        in_specs=in_specs,
        out_specs=pl.BlockSpec((None, tm, tn), lambda b, i, j: (b, i, j)),
        out_shape=jax.ShapeDtypeStruct((bsz, r_all, n), out_dtype),
        compiler_params=_params("parallel", "parallel", "arbitrary"),
        name=name,
    )(x3, w, *[a for a, _ in extras])


def _merge_epilogue(acc_b, bra_ref, mga_ref, mgb_ref):
    return _sigmoid(mga_ref[...]) * bra_ref[...] + _sigmoid(mgb_ref[...]) * acc_b


def _residual_epilogue(acc, h_ref, gate_ref, gate_c_ref, *, tm, n_lat):
    row = pl.program_id(1) * tm + lax.broadcasted_iota(jnp.int32, (tm, 1), 0)
    gate = jnp.where(row < n_lat, gate_ref[...], gate_c_ref[...])
    return h_ref[...] + gate * acc


Q_HEADS_PER_TILE = 4


def _q_kernel(cq_ref, g_ref, wa_ref, wb_ref, cos_ref, sin_ref, o_ref, xn_scr):
    @pl.when(pl.program_id(2) == 0)
    def _():
        x = cq_ref[...]
        ms = jnp.mean(x * x, axis=-1, keepdims=True)
        xn_scr[...] = (x * lax.rsqrt(ms + EPS) * g_ref[...]).astype(xn_scr.dtype)

    xn = xn_scr[...]
    a = _dot(xn, wa_ref[...])
    sw = _dot(xn, wb_ref[...])
    cos = cos_ref[...]
    sin = sin_ref[...]
    for hh in range(Q_HEADS_PER_TILE):
        base = hh * QK_PAD
        o_ref[:, base:base + LANES] = (a[:, base:base + LANES] * Q_SCALE).astype(o_ref.dtype)
        rope = a[:, base + LANES:base + QK_PAD] * cos + sw[:, hh * LANES:(hh + 1) * LANES] * sin
        o_ref[:, base + LANES:base + QK_PAD] = (rope * Q_SCALE).astype(o_ref.dtype)


def _q_proj(p, g_q, wqa, wqb, cos_t, sin_t, tm):
    bsz, rows, _ = p.shape
    tn = Q_HEADS_PER_TILE * QK_PAD
    n = MLA_HEADS * QK_PAD
    return pl.pallas_call(
        _q_kernel,
        grid=(bsz, rows // tm, n // tn),
        in_specs=[pl.BlockSpec((None, tm, Q_RANK), lambda b, i, j: (b, i, OFF_CQ // Q_RANK)),
                  pl.BlockSpec((1, Q_RANK), lambda b, i, j: (0, 0)),
                  pl.BlockSpec((Q_RANK, tn), lambda b, i, j: (0, j)),
                  pl.BlockSpec((Q_RANK, tn // 2), lambda b, i, j: (0, j)),
                  pl.BlockSpec((tm, LANES), lambda b, i, j: (i, 0)),
                  pl.BlockSpec((tm, LANES), lambda b, i, j: (i, 0))],
        out_specs=pl.BlockSpec((None, tm, tn), lambda b, i, j: (b, i, j)),
        out_shape=jax.ShapeDtypeStruct((bsz, rows, n), MXU_DTYPE),
        scratch_shapes=[pltpu.VMEM((tm, Q_RANK), MXU_DTYPE)],
        compiler_params=_params("parallel", "parallel", "arbitrary"),
        name="q_up",
    )(p, g_q.reshape(1, Q_RANK), wqa, wqb, cos_t, sin_t)


def _kv_kernel(ckv_ref, g_ref, wk_ref, wvt_ref, kpe_ref, kpesw_ref, cos_ref, sin_ref,
               k_ref, vt_ref, xn_scr, kp_scr):
    @pl.when(pl.program_id(2) == 0)
    def _():
        x = ckv_ref[...]
        ms = jnp.mean(x * x, axis=-1, keepdims=True)
        xn_scr[...] = (x * lax.rsqrt(ms + EPS) * g_ref[...]).astype(xn_scr.dtype)
        kp_scr[...] = (kpe_ref[...] * cos_ref[...] + kpesw_ref[...] * sin_ref[...]).astype(kp_scr.dtype)

    xn = xn_scr[...]
    kn = _dot(xn, wk_ref[...])
    vt = _dot_nt(wvt_ref[...], xn)
    sub = lax.broadcasted_iota(jnp.int32, (VT_ROWS - V_DIM, xn.shape[0]), 0)
    one_row = jnp.where(sub == 0, 1.0, 0.0).astype(vt_ref.dtype)
    for hh in range(Q_HEADS_PER_TILE):
        base = hh * QK_PAD
        k_ref[:, base:base + LANES] = kn[:, hh * LANES:(hh + 1) * LANES].astype(k_ref.dtype)
        k_ref[:, base + LANES:base + QK_PAD] = kp_scr[...]
        vt_ref[hh, 0:V_DIM, :] = vt[hh * V_DIM:(hh + 1) * V_DIM].astype(vt_ref.dtype)
        vt_ref[hh, V_DIM:VT_ROWS, :] = one_row


def _kv_proj(p, g_kv, wk, wvt, cos_t, sin_t, tm):
    bsz, rows, _ = p.shape
    assert rows % tm == 0 and tm % LANES == 0
    tk = Q_HEADS_PER_TILE * QK_PAD
    tw = Q_HEADS_PER_TILE * V_DIM
    nk = MLA_HEADS * QK_PAD
    return pl.pallas_call(
        _kv_kernel,
        grid=(bsz, rows // tm, nk // tk),
        in_specs=[pl.BlockSpec((None, tm, KV_RANK), lambda b, i, j: (b, i, OFF_CKV // KV_RANK)),
                  pl.BlockSpec((1, KV_RANK), lambda b, i, j: (0, 0)),
                  pl.BlockSpec((KV_RANK, tw), lambda b, i, j: (0, j)),
                  pl.BlockSpec((tw, KV_RANK), lambda b, i, j: (j, 0)),
                  pl.BlockSpec((None, tm, LANES), lambda b, i, j: (b, i, OFF_KPE // LANES)),
                  pl.BlockSpec((None, tm, LANES), lambda b, i, j: (b, i, OFF_KPESW // LANES)),
                  pl.BlockSpec((tm, LANES), lambda b, i, j: (i, 0)),
                  pl.BlockSpec((tm, LANES), lambda b, i, j: (i, 0))],
        out_specs=[pl.BlockSpec((None, tm, tk), lambda b, i, j: (b, i, j)),
                   pl.BlockSpec((None, Q_HEADS_PER_TILE, VT_ROWS, tm), lambda b, i, j: (b, j, 0, i))],
        out_shape=[jax.ShapeDtypeStruct((bsz, rows, nk), MXU_DTYPE),
                   jax.ShapeDtypeStruct((bsz, MLA_HEADS, VT_ROWS, rows), MXU_DTYPE)],
        scratch_shapes=[pltpu.VMEM((tm, KV_RANK), MXU_DTYPE), pltpu.VMEM((tm, LANES), MXU_DTYPE)],
        compiler_params=_params("parallel", "parallel", "arbitrary"),
        name="kv_up",
    )(p, g_kv.reshape(1, KV_RANK), wk, wvt, p, p, cos_t, sin_t)


ATT_TQ = 1024
ATT_TQ_CTX = 256
ATT_TK = 1024


def _key_chunk(key_rows):
    best = None
    for t in range(QK_PAD, min(key_rows, ATT_TK) + 1, QK_PAD):
        if key_rows % t == 0:
            best = t
    assert best is not None, key_rows
    return best


def _attn_kernel(q_ref, k_ref, vt_ref, ga_ref, o_ref, m_scr, acc_scr, s_a, s_b, s_c, s_d,
                 *, n_chunks, tk):
    q = q_ref[...]
    m_scr[...] = jnp.full(m_scr.shape, -jnp.inf, F32)
    acc_scr[...] = jnp.zeros(acc_scr.shape, F32)

    def rows_of(c):
        r0 = c * tk
        return r0 if isinstance(r0, int) else pl.multiple_of(r0, tk)

    def scores(c, dst):
        dst[...] = _dot_nt(k_ref[pl.ds(rows_of(c), tk), :], q)

    def absorb(c, src):
        s = src[...]
        m_old = m_scr[...]
        m_new = jnp.maximum(m_old, jnp.max(s, axis=0, keepdims=True))
        p = jnp.exp2(s - m_new).astype(vt_ref.dtype)
        pv = _dot(vt_ref[:, pl.ds(rows_of(c), tk)], p)
        acc_scr[...] = jnp.exp2(m_old - m_new) * acc_scr[...] + pv
        m_scr[...] = m_new

    if n_chunks == 1:
        scores(0, s_a)
        absorb(0, s_a)
    else:
        scores(0, s_a)
        scores(1, s_b)
        n_it = (n_chunks - 2) // 2

        def step(c, src, dst):
            scores(c + 2, dst[0])
            absorb(c, src[0])
            scores(c + 3, dst[1])
            absorb(c + 1, src[1])

        def body(j, carry):
            c = 2 * j
            lax.cond(j % 2 == 0,
                     lambda: step(c, (s_a, s_b), (s_c, s_d)),
                     lambda: step(c, (s_c, s_d), (s_a, s_b)))
            return carry

        if n_it:
            lax.fori_loop(0, n_it, body, 0)
        c = 2 * n_it
        src, dst = ((s_a, s_b), (s_c, s_d)) if n_it % 2 == 0 else ((s_c, s_d), (s_a, s_b))
        if n_chunks - c == 3:
            scores(c + 2, dst[0])
        absorb(c, src[0])
        absorb(c + 1, src[1])
        if n_chunks - c == 3:
            absorb(c + 2, dst[0])
    acc = acc_scr[...]
    o_t = acc[:V_DIM] / acc[V_DIM:V_DIM + 1]
    o_ref[...] = (o_t.T * _silu(ga_ref[...])).astype(o_ref.dtype)


def _attn_kernel_aliased(q_ref, k_ref, vt_ref, ga_ref, prev_ref, o_ref, *scratch, **kw):
    del prev_ref
    _attn_kernel(q_ref, k_ref, vt_ref, ga_ref, o_ref, *scratch, **kw)


def _attention(q, k, vt, p, *, tq, q_row0, q_rows, key_row0, key_rows, prev=None):
    bsz, rows, _ = q.shape
    assert q_row0 % tq == 0 and q_rows % tq == 0 and key_row0 % key_rows == 0
    qb, kb = q_row0 // tq, key_row0 // key_rows
    tk = _key_chunk(key_rows)
    kw = dict(n_chunks=key_rows // tk, tk=tk)
    in_specs = [pl.BlockSpec((None, tq, QK_PAD), lambda b, h, i: (b, qb + i, h)),
                pl.BlockSpec((None, key_rows, QK_PAD), lambda b, h, i: (b, kb, h)),
                pl.BlockSpec((None, None, VT_ROWS, key_rows), lambda b, h, i: (b, h, 0, kb)),
                pl.BlockSpec((None, tq, V_DIM), lambda b, h, i: (b, qb + i, OFF_GA // V_DIM + h))]
    args = [q, k, vt, p]
    if prev is None:
        kern, aliases, name = functools.partial(_attn_kernel, **kw), {}, "attention"
    else:
        kern, aliases, name = functools.partial(_attn_kernel_aliased, **kw), {4: 0}, "attention_ctx"
        in_specs.append(pl.BlockSpec(memory_space=pl.ANY))
        args.append(prev)
    return pl.pallas_call(
        kern,
        grid=(bsz, MLA_HEADS, q_rows // tq),
        in_specs=in_specs,
        out_specs=pl.BlockSpec((None, tq, V_DIM), lambda b, h, i: (b, qb + i, h)),
        out_shape=jax.ShapeDtypeStruct((bsz, rows, MLA_WIDTH), MXU_DTYPE),
        scratch_shapes=[pltpu.VMEM((1, tq), F32), pltpu.VMEM((VT_ROWS, tq), F32),
                        *[pltpu.VMEM((tk, tq), F32) for _ in range(4)]],
        input_output_aliases=aliases,
        compiler_params=_params("parallel", "parallel", "arbitrary"),
        name=name,
    )(*args)


CONV_TC = 2048
CONV_TR = 2 * CHUNK
HALO = SUBLANES


def _conv_kernel(prev_ref, cur_ref, next_ref, w_ref, b_ref, o_ref, ext, *, lat_tiles, all_tiles):
    c = pl.program_id(1)
    has_left = jnp.logical_and(c != 0, c != lat_tiles)
    has_right = jnp.logical_and(c != lat_tiles - 1, c != all_tiles - 1)
    ext[0:HALO, :] = jnp.where(has_left, prev_ref[...], 0.0)
    ext[HALO:HALO + CONV_TR, :] = cur_ref[...]
    ext[HALO + CONV_TR:, :] = jnp.where(has_right, next_ref[...], 0.0)
    pad = (CONV_K - 1) // 2
    acc = jnp.broadcast_to(b_ref[...], (CONV_TR, CONV_TC))
    for kk in range(CONV_K):
        lo = HALO - pad + kk
        acc = acc + w_ref[kk:kk + 1, :] * ext[lo:lo + CONV_TR, :]
    o_ref[...] = _silu(acc)


def _conv(p, w8, bias, n_lat):
    bsz, rows, _ = p.shape
    all_tiles = rows // CONV_TR
    lat_tiles = n_lat // CONV_TR
    per = CONV_TR // HALO
    cb = OFF_X // CONV_TC
    last_halo = rows // HALO - 1
    kern = functools.partial(_conv_kernel, lat_tiles=lat_tiles, all_tiles=all_tiles)
    return pl.pallas_call(
        kern,
        grid=(bsz, all_tiles, CONV_CH // CONV_TC),
        in_specs=[pl.BlockSpec((None, HALO, CONV_TC),
                               lambda b, c, j: (b, jnp.maximum(c * per - 1, 0), cb + j)),
                  pl.BlockSpec((None, CONV_TR, CONV_TC), lambda b, c, j: (b, c, cb + j)),
                  pl.BlockSpec((None, HALO, CONV_TC),
                               lambda b, c, j: (b, jnp.minimum((c + 1) * per, last_halo), cb + j)),
                  pl.BlockSpec((SUBLANES, CONV_TC), lambda b, c, j: (0, j)),
                  pl.BlockSpec((1, CONV_TC), lambda b, c, j: (0, j))],
        out_specs=pl.BlockSpec((None, CONV_TR, CONV_TC), lambda b, c, j: (b, c, j)),
        out_shape=jax.ShapeDtypeStruct((bsz, rows, CONV_CH), F32),
        scratch_shapes=[pltpu.VMEM((CONV_TR + 2 * HALO, CONV_TC), F32)],
        compiler_params=_params("parallel", "parallel", "arbitrary"),
        name="dwconv",
    )(p, p, p, w8, bias.reshape(1, CONV_CH))


def _ssd_kernel(*refs, reverse):
    if reverse:
        (x_ref, b_ref, c_ref, dt_ref, bias_ref, alog_ref, e2_ref,
         yf_ref, z_ref, dsk_ref, gs_ref, o_ref, state, rrow, ybuf) = refs
        dest = ybuf
    else:
        (x_ref, b_ref, c_ref, dt_ref, bias_ref, alog_ref, e2_ref, o_ref, state, rrow) = refs
        dest = o_ref
    L = CHUNK
    half = PAIR_W // 2

    @pl.when(pl.program_id(1) == 0)
    def _():
        state[...] = jnp.zeros_like(state)

    off = SSM_HEADS if reverse else 0
    dt = _softplus(dt_ref[:, off:off + SSM_HEADS] + bias_ref[...])
    a = dt * (-jnp.exp(alog_ref[...]))

    ri = lax.broadcasted_iota(jnp.int32, (L, L), 0)
    ci = lax.broadcasted_iota(jnp.int32, (L, L), 1)
    tri = jnp.where((ri <= ci) if reverse else (ri >= ci), 1.0, 0.0).astype(MXU_DTYPE)
    p1 = a.astype(MXU_DTYPE)
    r1 = a - p1.astype(F32)
    p2 = r1.astype(MXU_DTYPE)
    p3 = (r1 - p2.astype(F32)).astype(MXU_DTYPE)
    acs = _dot(tri, p1) + _dot(tri, p2) + _dot(tri, p3)

    acs_t = acs.T
    lane_h = lax.broadcasted_iota(jnp.int32, (SSM_HEADS // 2, LANES), 1)
    ev = acs_t[0:SSM_HEADS // 2]
    od = acs_t[SSM_HEADS // 2:]
    rrow[0] = jnp.where(lane_h < half, ev, pltpu.roll(od, half, 1))
    rrow[1] = jnp.where(lane_h < half, pltpu.roll(ev, half, 1), od)

    acs_hl = jnp.concatenate(_split_hi_lo(acs), axis=1)
    dt_hl = jnp.concatenate(_split_hi_lo(dt), axis=1)
    edge = 0 if reverse else L - 1
    far_block = 0 if reverse else 1

    li = lax.broadcasted_iota(jnp.int32, (L, LANES), 0)
    lane = lax.broadcasted_iota(jnp.int32, (L, LANES), 1)
    src = jnp.where(lane < half, lane, lane - half)
    masks = []
    for sb in range(2):
        s_pos = src + sb * half
        masks.append((li <= s_pos) if reverse else (li >= s_pos))
    lane_s = lax.broadcasted_iota(jnp.int32, (half, LANES), 1)
    first_head = lane_s < half

    def group(g, carry):
        c0 = pl.multiple_of(g * GROUP_W, GROUP_W)
        n0 = pl.multiple_of(g * SSM_STATE, SSM_STATE)
        e2g = e2_ref[:, pl.ds(c0, GROUP_W)]
        acs_x = _dot(acs_hl, e2g)
        dt_x = _dot(dt_hl, e2g)
        xdt = x_ref[:, pl.ds(c0, GROUP_W)] * dt_x
        bg = b_ref[:, pl.ds(n0, SSM_STATE)]
        cg = c_ref[:, pl.ds(n0, SSM_STATE)].astype(MXU_DTYPE)
        bgb = bg.astype(MXU_DTYPE)
        last_x = acs_x[edge:edge + 1, :]

        sg = state[g]
        y_off = _dot(cg, sg.astype(MXU_DTYPE)) * jnp.exp(acs_x)
        wgt = (xdt * jnp.exp(last_x - acs_x)).astype(MXU_DTYPE)
        state[g] = sg * jnp.exp(last_x) + _dot(bg.T.astype(MXU_DTYPE), wgt)

        cbd = []
        for sb in range(2):
            bs = bgb[sb * half:(sb + 1) * half]
            cbd.append(_dot_nt(cg, jnp.concatenate([bs, bs], axis=0)))

        for jj in range(GROUP_W // PAIR_W):
            col = acs_x[:, jj * PAIR_W:(jj + 1) * PAIR_W]
            yp = y_off[:, jj * PAIR_W:(jj + 1) * PAIR_W]
            for sb in range(2):
                row = rrow[sb, pl.ds(g * (GROUP_W // PAIR_W) + jj, 1), :]
                xs = xdt[sb * half:(sb + 1) * half, jj * PAIR_W:(jj + 1) * PAIR_W]
                xbd = jnp.concatenate([jnp.where(first_head, xs, 0.0),
                                       jnp.where(first_head, 0.0, xs)], axis=0).astype(MXU_DTYPE)
                if sb == far_block:
                    t0 = sb * half
                    seg = jnp.where(masks[sb][t0:t0 + half], col[t0:t0 + half] - row, -jnp.inf)
                    mp = (cbd[sb][t0:t0 + half] * jnp.exp(seg)).astype(MXU_DTYPE)
                    upd = _dot(mp, xbd)
                    parts = [yp[:half], yp[half:]]
                    parts[sb] = parts[sb] + upd
                    yp = jnp.concatenate(parts, axis=0)
                else:
                    lm = jnp.exp(jnp.where(masks[sb], col - row, -jnp.inf))
                    mp = (cbd[sb] * lm).astype(MXU_DTYPE)
                    yp = yp + _dot(mp, xbd)
            dest[:, pl.ds(pl.multiple_of(c0 + jj * PAIR_W, PAIR_W), PAIR_W)] = yp
        return carry

    lax.fori_loop(0, SSM_GROUPS, group, 0, unroll=2)

    if reverse:
        ssq = jnp.zeros((L, 1), F32)
        for g in range(SSM_GROUPS):
            sl = slice(g * GROUP_W, (g + 1) * GROUP_W)
            y = ybuf[:, sl] + yf_ref[:, sl] + x_ref[:, sl] * dsk_ref[:, sl]
            yz = y * _silu(z_ref[:, sl])
            ybuf[:, sl] = yz
            ssq = ssq + jnp.sum(yz * yz, axis=-1, keepdims=True)
        r = lax.rsqrt(ssq / SSM_INNER + EPS)
        for g in range(SSM_GROUPS):
            sl = slice(g * GROUP_W, (g + 1) * GROUP_W)
            o_ref[:, sl] = (ybuf[:, sl] * r * gs_ref[:, sl]).astype(o_ref.dtype)


def _ssd(xbc, p, bias, alog, e2, n_lat, *, reverse, yf=None, dskip_x=None, g_ssm=None):
    bsz, rows, _ = xbc.shape
    all_chunks = rows // CHUNK
    lat_chunks = n_lat // CHUNK
    ctx_chunks = all_chunks - lat_chunks

    if reverse:
        def chunk(s):
            return all_chunks - 1 - s
    else:
        def chunk(s):
            return jnp.where(s < ctx_chunks, lat_chunks + s, s - ctx_chunks)

    row_spec = lambda width, blk: pl.BlockSpec((None, CHUNK, width), lambda b, s: (b, chunk(s), blk))
    const = lambda shape: pl.BlockSpec(shape, lambda b, s: (0,) * len(shape))
    ng = SSM_GROUPS * SSM_STATE
    in_specs = [row_spec(SSM_INNER, 0),
                row_spec(ng, SSM_INNER // ng),
                row_spec(ng, SSM_INNER // ng + 1),
                row_spec(2 * SSM_HEADS, OFF_DT // (2 * SSM_HEADS)),
                const((1, SSM_HEADS)), const((1, SSM_HEADS)), const((2 * SSM_HEADS, SSM_INNER))]
    args = [xbc, xbc, xbc, p, bias.reshape(1, SSM_HEADS), alog.reshape(1, SSM_HEADS), e2]
    scratch = [pltpu.VMEM((SSM_GROUPS, SSM_STATE, GROUP_W), F32),
               pltpu.VMEM((2, SSM_HEADS // 2, LANES), F32)]
    if reverse:
        in_specs += [row_spec(SSM_INNER, 0), row_spec(SSM_INNER, OFF_Z // SSM_INNER),
                     const((1, SSM_INNER)), const((1, SSM_INNER))]
        args += [yf, p, dskip_x.reshape(1, SSM_INNER), g_ssm.reshape(1, SSM_INNER)]
        scratch.append(pltpu.VMEM((CHUNK, SSM_INNER), F32))
        out_dtype = MXU_DTYPE
    else:
        out_dtype = F32
    return pl.pallas_call(
        functools.partial(_ssd_kernel, reverse=reverse),
        grid=(bsz, all_chunks),
        in_specs=in_specs,
        out_specs=row_spec(SSM_INNER, 0),
        out_shape=jax.ShapeDtypeStruct((bsz, rows, SSM_INNER), out_dtype),
        scratch_shapes=scratch,
        compiler_params=_params("parallel", "arbitrary"),
        name="ssd_bwd" if reverse else "ssd_fwd",
    )(*args)


def _head_perm():
    k = jnp.arange(SSM_HEADS)
    return jnp.where(k < SSM_HEADS // 2, 2 * k, 2 * (k - SSM_HEADS // 2) + 1)


def _prep_w_in(w):
    sizes = (Q_RANK, KV_RANK, QK_ROPE, MLA_WIDTH, SSM_INNER, SSM_INNER,
             SSM_GROUPS * SSM_STATE, SSM_GROUPS * SSM_STATE, SSM_HEADS, SSM_HEADS, D_MODEL, D_MODEL)
    offs = [0]
    for s in sizes:
        offs.append(offs[-1] + s)
    (cq, ckv, kpe, ga, z, xs, bs, cs, dtf, dtb, mga, mgb) = [w[:, offs[i]:offs[i + 1]] for i in range(12)]
    perm = _head_perm()
    zpad = jnp.zeros((w.shape[0], LANES - QK_ROPE), w.dtype)
    k1, k2 = kpe[:, :QK_ROPE // 2], kpe[:, QK_ROPE // 2:]
    cols = [z, xs, bs, cs, ga, mga, mgb, ckv, dtf[:, perm], dtb[:, perm], cq,
            kpe, zpad, -k2, k1, zpad]
    used = sum(c.shape[1] for c in cols)
    cols.append(jnp.zeros((w.shape[0], IN_PAD - used), w.dtype))
    return jnp.concatenate(cols, axis=1).astype(MXU_DTYPE)


def _prep_w_uq(w):
    w3 = w.reshape(Q_RANK, MLA_HEADS, QK_DIM)
    nope = w3[..., :QK_NOPE]
    r1 = w3[..., QK_NOPE:QK_NOPE + QK_ROPE // 2]
    r2 = w3[..., QK_NOPE + QK_ROPE // 2:]
    zp = jnp.zeros((Q_RANK, MLA_HEADS, LANES - QK_ROPE), w.dtype)
    wa = jnp.concatenate([nope, r1, r2, zp], axis=-1).reshape(Q_RANK, MLA_HEADS * QK_PAD)
    wb = jnp.concatenate([-r2, r1, zp], axis=-1).reshape(Q_RANK, MLA_HEADS * LANES)
    return wa.astype(MXU_DTYPE), wb.astype(MXU_DTYPE)


def _prep_w_ukv(w):
    w3 = w.reshape(KV_RANK, MLA_HEADS, QK_NOPE + V_DIM)
    wk = w3[..., :QK_NOPE].reshape(KV_RANK, MLA_HEADS * QK_NOPE)
    wvt = w3[..., QK_NOPE:].reshape(KV_RANK, MLA_HEADS * V_DIM).T
    return wk.astype(MXU_DTYPE), wvt.astype(MXU_DTYPE)


def _rope_tables(n_lat, n_ctx):
    n_rows = n_lat // GRID_W
    rows = jnp.broadcast_to(jnp.arange(n_rows, dtype=F32)[:, None], (n_rows, GRID_W)).reshape(-1)
    cols = jnp.broadcast_to(jnp.arange(GRID_W, dtype=F32)[None, :], (n_rows, GRID_W)).reshape(-1)
    n_freq = QK_ROPE // 4
    inv = ROPE_BASE ** (-jnp.arange(n_freq, dtype=F32) / n_freq)
    ang = jnp.concatenate([rows[:, None] * inv, cols[:, None] * inv], axis=-1)
    cos, sin = jnp.cos(ang), jnp.sin(ang)
    ones = jnp.ones((n_lat, LANES - QK_ROPE), F32)
    cos_t = jnp.concatenate([cos, cos, ones], axis=1)
    sin_t = jnp.concatenate([sin, sin, 0.0 * ones], axis=1)
    cos_t = jnp.concatenate([cos_t, jnp.ones((n_ctx, LANES), F32)], axis=0)
    sin_t = jnp.concatenate([sin_t, jnp.zeros((n_ctx, LANES), F32)], axis=0)
    return cos_t, sin_t


def _expand_matrix():
    head = _head_perm()[:, None]
    col_head = (jnp.arange(SSM_INNER) // SSM_HEADDIM)[None, :]
    e = jnp.where(head == col_head, 1.0, 0.0).astype(MXU_DTYPE)
    return jnp.concatenate([e, e], axis=0)


def kernel(x, c, ctx, c_ctx, w_ada, b_ada, g_pre, w_in, g_q, w_uq, g_kv, w_ukv, conv_w, conv_b,
           dt_bias_f, dt_bias_b, a_log_f, a_log_b, d_skip, g_ssm, w_proj_a, w_proj_b, w_out, g_final):
    bsz, n_lat, d = x.shape
    n_ctx = ctx.shape[1]
    depth = w_in.shape[0]
    rows = n_lat + n_ctx
    assert d == D_MODEL and n_lat % GRID_W == 0 and n_lat % n_ctx == 0 and n_ctx % (2 * CHUNK) == 0
    assert bsz + 1 <= SUBLANES

    hs = jnp.concatenate([x, ctx], axis=1)
    cc = jnp.zeros((SUBLANES, d), F32).at[:bsz].set(c).at[bsz].set(c_ctx)
    cos_t, sin_t = _rope_tables(n_lat, n_ctx)
    e2 = _expand_matrix()
    perm = _head_perm()
    tm_all = _row_tile(rows, 1056)

    for i in range(depth):
        last = i == depth - 1
        out_rows = n_lat if last else rows
        tm_out = _row_tile(out_rows, 1056)
        tm_half = _row_tile(out_rows, 528)

        w_in_r = _prep_w_in(w_in[i])
        wqa, wqb = _prep_w_uq(w_uq[i])
        wk, wvt = _prep_w_ukv(w_ukv[i])
        wa = w_proj_a[i].astype(MXU_DTYPE)
        wb = w_proj_b[i].astype(MXU_DTYPE)
        wo = w_out[i].astype(MXU_DTYPE)
        conv_w8 = jnp.zeros((SUBLANES, CONV_CH), F32).at[:CONV_K].set(conv_w[i])

        mod = _ada(cc, w_ada[i], b_ada[i])
        mod3 = mod.reshape(SUBLANES, 1, 3 * d)
        u = _norm_mod(hs, g_pre[i], mod3, n_lat)
        p = _matmul(u, w_in_r, rows=rows, tm=tm_all, tn=1024, out_dtype=F32, name="in_proj")

        q = _q_proj(p, g_q[i], wqa, wqb, cos_t, sin_t, tm_all)
        k, vt = _kv_proj(p, g_kv[i], wk, wvt, cos_t, sin_t, _lane_tile(rows, 1024))
        oa = _attention(q, k, vt, p, tq=_row_tile(n_lat, ATT_TQ), q_row0=0, q_rows=n_lat,
                        key_row0=0, key_rows=rows)
        if not last:
            oa = _attention(q, k, vt, p, tq=_row_tile(n_ctx, ATT_TQ_CTX), q_row0=n_lat, q_rows=n_ctx,
                            key_row0=n_lat, key_rows=n_ctx, prev=oa)
        bra = _matmul(oa, wa, rows=out_rows, tm=tm_out, tn=1024, out_dtype=F32, name="proj_a")

        xbc = _conv(p, conv_w8, conv_b[i], n_lat)
        yf = _ssd(xbc, p, dt_bias_f[i][perm], a_log_f[i][perm], e2, n_lat, reverse=False)
        yn = _ssd(xbc, p, dt_bias_b[i][perm], a_log_b[i][perm], e2, n_lat, reverse=True,
                  yf=yf, dskip_x=jnp.repeat(d_skip[i], SSM_HEADDIM), g_ssm=g_ssm[i])

        tn_m = 512
        tile = lambda blk0: pl.BlockSpec((None, tm_half, tn_m), lambda b, r, j: (b, r, blk0 + j))
        merged = _matmul(
            yn, wb, rows=out_rows, tm=tm_half, tn=tn_m, out_dtype=MXU_DTYPE,
            epilogue=_merge_epilogue,
            extras=[(bra, tile(0)), (p, tile(OFF_MGA // tn_m)), (p, tile(OFF_MGB // tn_m))],
            name="proj_b_merge")
        gate_spec = lambda row: pl.BlockSpec((None, 1, tn_m), lambda b, r, j: (row(b), 0, 2 * (d // tn_m) + j))
        hs = _matmul(
            merged, wo, rows=out_rows, tm=tm_out, tn=tn_m, out_dtype=F32,
            epilogue=functools.partial(_residual_epilogue, tm=tm_out, n_lat=n_lat),
            extras=[(hs, pl.BlockSpec((None, tm_out, tn_m), lambda b, r, j: (b, r, j))),
                    (mod3, gate_spec(lambda b: b)), (mod3, gate_spec(lambda b: bsz))],
            name="out_proj")

    return _final_norm(hs, g_final, n_lat)
```

```python
import functools
import math

import jax
import jax.numpy as jnp
from jax import lax
from jax.experimental import pallas as pl
from jax.experimental.pallas import tpu as pltpu

F32 = jnp.float32
MXU_DTYPE = jnp.bfloat16

D_MODEL = 4096
GRID_W = 64
MLA_HEADS = 32
QK_NOPE = 128
QK_ROPE = 64
V_DIM = 128
Q_RANK = 768
KV_RANK = 512
MLA_WIDTH = MLA_HEADS * V_DIM
QK_DIM = QK_NOPE + QK_ROPE
ATTN_SCALE = 1.0 / math.sqrt(QK_DIM)
Q_SCALE = ATTN_SCALE * math.log2(math.e)
ROPE_BASE = 10000.0
SSM_INNER = 2 * D_MODEL
SSM_HEADDIM = 64
SSM_HEADS = SSM_INNER // SSM_HEADDIM
SSM_GROUPS = 8
SSM_HPG = SSM_HEADS // SSM_GROUPS
SSM_STATE = 128
CONV_K = 5
CONV_CH = SSM_INNER + 2 * SSM_GROUPS * SSM_STATE
CHUNK = 128
EPS = 1e-6

LANES = 128
SUBLANES = 8
VMEM_LIMIT_BYTES = 56 * 1024 * 1024

QK_PAD = 2 * LANES
VT_ROWS = V_DIM + 16
GROUP_W = SSM_HPG * SSM_HEADDIM
PAIR_W = 2 * SSM_HEADDIM

OFF_Z = 0
OFF_X = OFF_Z + SSM_INNER
OFF_B = OFF_X + SSM_INNER
OFF_C = OFF_B + SSM_GROUPS * SSM_STATE
OFF_GA = OFF_X + CONV_CH
OFF_MGA = OFF_GA + MLA_WIDTH
OFF_MGB = OFF_MGA + D_MODEL
OFF_CKV = OFF_MGB + D_MODEL
OFF_DT = OFF_CKV + KV_RANK
OFF_CQ = OFF_DT + 2 * SSM_HEADS
OFF_KPE = OFF_CQ + Q_RANK
OFF_KPESW = OFF_KPE + LANES
IN_PAD = 32768
assert OFF_KPESW + LANES <= IN_PAD
assert OFF_CQ % Q_RANK == 0 and OFF_CKV % KV_RANK == 0 and OFF_DT % (2 * SSM_HEADS) == 0


def _sigmoid(x):
    return 1.0 / (1.0 + jnp.exp(-x))


def _silu(x):
    return x * _sigmoid(x)


def _softplus(x):
    return jnp.maximum(x, 0.0) + jnp.log1p(jnp.exp(-jnp.abs(x)))


def _dot(a, b):
    return jnp.dot(a, b, preferred_element_type=F32)


def _dot_nt(a, b):
    return lax.dot_general(a, b, (((1,), (1,)), ((), ())), preferred_element_type=F32)


def _split_hi_lo(x):
    hi = x.astype(MXU_DTYPE)
    lo = (x - hi.astype(F32)).astype(MXU_DTYPE)
    return hi, lo


def _row_tile(rows, target):
    best = None
    for t in range(16, min(rows, target) + 1, 16):
        if rows % t == 0:
            best = t
    assert best is not None, rows
    return best


def _lane_tile(rows, target):
    best = None
    for t in range(LANES, min(rows, target) + 1, LANES):
        if rows % t == 0:
            best = t
    assert best is not None, rows
    return best


def _params(*sem):
    return pltpu.CompilerParams(dimension_semantics=sem, vmem_limit_bytes=VMEM_LIMIT_BYTES)


def _cast_kernel(w_ref, o_ref):
    o_ref[...] = w_ref[...].astype(o_ref.dtype)


def _cast_layer(w_stack, layer):
    _, k, n = w_stack.shape
    tr = _row_tile(k, 512)
    return pl.pallas_call(
        _cast_kernel,
        grid=(k // tr,),
        in_specs=[pl.BlockSpec((None, tr, n), lambda r: (layer, r, 0))],
        out_specs=pl.BlockSpec((tr, n), lambda r: (r, 0)),
        out_shape=jax.ShapeDtypeStruct((k, n), MXU_DTYPE),
        compiler_params=_params("arbitrary"),
        name="cast_weight",
    )(w_stack)


def _ada_kernel(c_ref, w_ref, b_ref, o_ref):
    s = _silu(c_ref[...]).astype(MXU_DTYPE)
    o_ref[...] = _dot(s, w_ref[...].astype(MXU_DTYPE)) + b_ref[...]


def _ada(cc, w, b):
    rows, d = cc.shape
    n = w.shape[1]
    tn = 512
    return pl.pallas_call(
        _ada_kernel,
        grid=(n // tn,),
        in_specs=[pl.BlockSpec((rows, d), lambda j: (0, 0)),
                  pl.BlockSpec((d, tn), lambda j: (0, j)),
                  pl.BlockSpec((1, tn), lambda j: (0, j))],
        out_specs=pl.BlockSpec((rows, tn), lambda j: (0, j)),
        out_shape=jax.ShapeDtypeStruct((rows, n), F32),
        compiler_params=_params("arbitrary"),
        name="ada_mod",
    )(cc, w, b.reshape(1, n))


def _norm_mod_kernel(h_ref, g_ref, sh_ref, sc_ref, o_ref):
    x = h_ref[...]
    ms = jnp.mean(x * x, axis=-1, keepdims=True)
    y = x * lax.rsqrt(ms + EPS) * g_ref[...]
    o_ref[...] = (y * (1.0 + sc_ref[...]) + sh_ref[...]).astype(o_ref.dtype)


def _norm_mod(hs, g, mod3, n_lat):
    bsz, rows, d = hs.shape
    tr = 256
    lat_tiles = n_lat // tr
    ctx_row = bsz

    def mod_row(b, i):
        return jnp.where(i < lat_tiles, b, ctx_row)

    return pl.pallas_call(
        _norm_mod_kernel,
        grid=(bsz, rows // tr),
        in_specs=[pl.BlockSpec((None, tr, d), lambda b, i: (b, i, 0)),
                  pl.BlockSpec((1, d), lambda b, i: (0, 0)),
                  pl.BlockSpec((None, 1, d), lambda b, i: (mod_row(b, i), 0, 0)),
                  pl.BlockSpec((None, 1, d), lambda b, i: (mod_row(b, i), 0, 1))],
        out_specs=pl.BlockSpec((None, tr, d), lambda b, i: (b, i, 0)),
        out_shape=jax.ShapeDtypeStruct((bsz, rows, d), MXU_DTYPE),
        compiler_params=_params("parallel", "arbitrary"),
        name="norm_mod",
    )(hs, g.reshape(1, d), mod3, mod3)


def _final_norm_kernel(h_ref, g_ref, o_ref):
    x = h_ref[...]
    ms = jnp.mean(x * x, axis=-1, keepdims=True)
    o_ref[...] = x * lax.rsqrt(ms + EPS) * g_ref[...]


def _final_norm(hs, g, n_lat):
    bsz, _, d = hs.shape
    tr = 256
    return pl.pallas_call(
        _final_norm_kernel,
        grid=(bsz, n_lat // tr),
        in_specs=[pl.BlockSpec((None, tr, d), lambda b, i: (b, i, 0)),
                  pl.BlockSpec((1, d), lambda b, i: (0, 0))],
        out_specs=pl.BlockSpec((None, tr, d), lambda b, i: (b, i, 0)),
        out_shape=jax.ShapeDtypeStruct((bsz, n_lat, d), F32),
        compiler_params=_params("parallel", "arbitrary"),
        name="final_norm",
    )(hs, g.reshape(1, d))


def _mm_kernel(*refs, n_extra, epilogue):
    x_ref, w_ref = refs[:2]
    extras = refs[2:2 + n_extra]
    o_ref = refs[2 + n_extra]
    acc = _dot(x_ref[...], w_ref[...])
    if epilogue is not None:
        acc = epilogue(acc, *extras)
    o_ref[...] = acc.astype(o_ref.dtype)


def _matmul(x3, w, *, rows, tm, tn, out_dtype, epilogue=None, extras=(), name):
    bsz, r_all, k = x3.shape
    n = w.shape[1]
    assert rows % tm == 0 and n % tn == 0
    in_specs = [pl.BlockSpec((None, tm, k), lambda b, i, j: (b, i, 0)),
                pl.BlockSpec((k, tn), lambda b, i, j: (0, j))]
    in_specs += [spec for _, spec in extras]
    return pl.pallas_call(
        functools.partial(_mm_kernel, n_extra=len(extras), epilogue=epilogue),
        grid=(bsz, rows // tm, n // tn),
        in_specs=in_specs,
        out_specs=pl.BlockSpec((None, tm, tn), lambda b, i, j: (b, i, j)),
        out_shape=jax.ShapeDtypeStruct((bsz, r_all, n), out_dtype),
        compiler_params=_params("parallel", "parallel", "arbitrary"),
        name=name,
    )(x3, w, *[a for a, _ in extras])


def _merge_epilogue(acc_b, bra_ref, mga_ref, mgb_ref):
    return _sigmoid(mga_ref[...]) * bra_ref[...] + _sigmoid(mgb_ref[...]) * acc_b


def _residual_epilogue(acc, h_ref, gate_ref, gate_c_ref, *, tm, n_lat):
    row = pl.program_id(1) * tm + lax.broadcasted_iota(jnp.int32, (tm, 1), 0)
    gate = jnp.where(row < n_lat, gate_ref[...], gate_c_ref[...])
    return h_ref[...] + gate * acc


Q_HEADS_PER_TILE = 4


def _q_kernel(cq_ref, g_ref, wa_ref, wb_ref, cos_ref, sin_ref, o_ref, xn_scr):
    @pl.when(pl.program_id(2) == 0)
    def _():
        x = cq_ref[...]
        ms = jnp.mean(x * x, axis=-1, keepdims=True)
        xn_scr[...] = (x * lax.rsqrt(ms + EPS) * g_ref[...]).astype(xn_scr.dtype)

    xn = xn_scr[...]
    a = _dot(xn, wa_ref[...])
    sw = _dot(xn, wb_ref[...])
    cos = cos_ref[...]
    sin = sin_ref[...]
    for hh in range(Q_HEADS_PER_TILE):
        base = hh * QK_PAD
        o_ref[:, base:base + LANES] = (a[:, base:base + LANES] * Q_SCALE).astype(o_ref.dtype)
        rope = a[:, base + LANES:base + QK_PAD] * cos + sw[:, hh * LANES:(hh + 1) * LANES] * sin
        o_ref[:, base + LANES:base + QK_PAD] = (rope * Q_SCALE).astype(o_ref.dtype)


def _q_proj(p, g_q, wqa, wqb, cos_t, sin_t, tm):
    bsz, rows, _ = p.shape
    tn = Q_HEADS_PER_TILE * QK_PAD
    n = MLA_HEADS * QK_PAD
    return pl.pallas_call(
        _q_kernel,
        grid=(bsz, rows // tm, n // tn),
        in_specs=[pl.BlockSpec((None, tm, Q_RANK), lambda b, i, j: (b, i, OFF_CQ // Q_RANK)),
                  pl.BlockSpec((1, Q_RANK), lambda b, i, j: (0, 0)),
                  pl.BlockSpec((Q_RANK, tn), lambda b, i, j: (0, j)),
                  pl.BlockSpec((Q_RANK, tn // 2), lambda b, i, j: (0, j)),
                  pl.BlockSpec((tm, LANES), lambda b, i, j: (i, 0)),
                  pl.BlockSpec((tm, LANES), lambda b, i, j: (i, 0))],
        out_specs=pl.BlockSpec((None, tm, tn), lambda b, i, j: (b, i, j)),
        out_shape=jax.ShapeDtypeStruct((bsz, rows, n), MXU_DTYPE),
        scratch_shapes=[pltpu.VMEM((tm, Q_RANK), MXU_DTYPE)],
        compiler_params=_params("parallel", "parallel", "arbitrary"),
        name="q_up",
    )(p, g_q.reshape(1, Q_RANK), wqa, wqb, cos_t, sin_t)


def _kv_kernel(ckv_ref, g_ref, wk_ref, wvt_ref, kpe_ref, kpesw_ref, cos_ref, sin_ref,
               k_ref, vt_ref, xn_scr, kp_scr):
    @pl.when(pl.program_id(2) == 0)
    def _():
        x = ckv_ref[...]
        ms = jnp.mean(x * x, axis=-1, keepdims=True)
        xn_scr[...] = (x * lax.rsqrt(ms + EPS) * g_ref[...]).astype(xn_scr.dtype)
        kp_scr[...] = (kpe_ref[...] * cos_ref[...] + kpesw_ref[...] * sin_ref[...]).astype(kp_scr.dtype)

    xn = xn_scr[...]
    kn = _dot(xn, wk_ref[...])
    vt = _dot_nt(wvt_ref[...], xn)
    sub = lax.broadcasted_iota(jnp.int32, (VT_ROWS - V_DIM, xn.shape[0]), 0)
    one_row = jnp.where(sub == 0, 1.0, 0.0).astype(vt_ref.dtype)
    for hh in range(Q_HEADS_PER_TILE):
        base = hh * QK_PAD
        k_ref[:, base:base + LANES] = kn[:, hh * LANES:(hh + 1) * LANES].astype(k_ref.dtype)
        k_ref[:, base + LANES:base + QK_PAD] = kp_scr[...]
        vt_ref[hh, 0:V_DIM, :] = vt[hh * V_DIM:(hh + 1) * V_DIM].astype(vt_ref.dtype)
        vt_ref[hh, V_DIM:VT_ROWS, :] = one_row


def _kv_proj(p, g_kv, wk, wvt, cos_t, sin_t, tm):
    bsz, rows, _ = p.shape
    assert rows % tm == 0 and tm % LANES == 0
    tk = Q_HEADS_PER_TILE * QK_PAD
    tw = Q_HEADS_PER_TILE * V_DIM
    nk = MLA_HEADS * QK_PAD
    return pl.pallas_call(
        _kv_kernel,
        grid=(bsz, rows // tm, nk // tk),
        in_specs=[pl.BlockSpec((None, tm, KV_RANK), lambda b, i, j: (b, i, OFF_CKV // KV_RANK)),
                  pl.BlockSpec((1, KV_RANK), lambda b, i, j: (0, 0)),
                  pl.BlockSpec((KV_RANK, tw), lambda b, i, j: (0, j)),
                  pl.BlockSpec((tw, KV_RANK), lambda b, i, j: (j, 0)),
                  pl.BlockSpec((None, tm, LANES), lambda b, i, j: (b, i, OFF_KPE // LANES)),
                  pl.BlockSpec((None, tm, LANES), lambda b, i, j: (b, i, OFF_KPESW // LANES)),
                  pl.BlockSpec((tm, LANES), lambda b, i, j: (i, 0)),
                  pl.BlockSpec((tm, LANES), lambda b, i, j: (i, 0))],
        out_specs=[pl.BlockSpec((None, tm, tk), lambda b, i, j: (b, i, j)),
                   pl.BlockSpec((None, Q_HEADS_PER_TILE, VT_ROWS, tm), lambda b, i, j: (b, j, 0, i))],
        out_shape=[jax.ShapeDtypeStruct((bsz, rows, nk), MXU_DTYPE),
                   jax.ShapeDtypeStruct((bsz, MLA_HEADS, VT_ROWS, rows), MXU_DTYPE)],
        scratch_shapes=[pltpu.VMEM((tm, KV_RANK), MXU_DTYPE), pltpu.VMEM((tm, LANES), MXU_DTYPE)],
        compiler_params=_params("parallel", "parallel", "arbitrary"),
        name="kv_up",
    )(p, g_kv.reshape(1, KV_RANK), wk, wvt, p, p, cos_t, sin_t)


ATT_TQ = 1024
ATT_TQ_CTX = 256
ATT_TK = 1024
ATT_QW = QK_PAD
ATT_AHEAD = 3


def _mxu_tile(n, limit):
    best = None
    for t in range(QK_PAD, min(n, limit) + 1, QK_PAD):
        if n % t == 0:
            best = t
    assert best is not None, n
    return best


def _attn_kernel(q_ref, k_ref, vt_ref, ga_ref, o_ref, m_scr, acc_scr, cmax_scr, *bufs, n_chunks, tk):
    tq = q_ref.shape[0]
    qw = min(tq, ATT_QW)
    m_scr[...] = jnp.full(m_scr.shape, -jnp.inf, F32)
    acc_scr[...] = jnp.zeros(acc_scr.shape, F32)
    slot = {id(b): i for i, b in enumerate(bufs)}

    def rows_of(c):
        r0 = c * tk
        return r0 if isinstance(r0, int) else pl.multiple_of(r0, tk)

    def scores(c, dst, n):
        cols = slice(n * qw, (n + 1) * qw)
        s = _dot_nt(k_ref[pl.ds(rows_of(c), tk), :], q_ref[cols, :])
        dst[:, cols] = s
        cmax_scr[slot[id(dst)], :, cols] = jnp.max(s, axis=0, keepdims=True)

    def absorb(c, src, n):
        cols = slice(n * qw, (n + 1) * qw)
        s = src[:, cols]
        m_old = m_scr[:, cols]
        m_new = jnp.maximum(m_old, cmax_scr[slot[id(src)], :, cols])
        p = jnp.exp2(s - m_new).astype(vt_ref.dtype)
        pv = _dot(vt_ref[:, pl.ds(rows_of(c), tk)], p)
        acc_scr[:, cols] = jnp.exp2(m_old - m_new) * acc_scr[:, cols] + pv
        m_scr[:, cols] = m_new

    def both(c_score, dst, c_absorb, src):
        for n in range(tq // qw):
            if dst is not None:
                scores(c_score, dst, n)
            if src is not None:
                absorb(c_absorb, src, n)

    ahead = len(bufs) // 2
    sets = (bufs[:ahead], bufs[ahead:])
    for t in range(min(ahead, n_chunks)):
        both(t, sets[0][t], None, None)
    n_it = max(n_chunks - ahead, 0) // ahead

    def step(c, src, dst):
        for t in range(ahead):
            both(c + ahead + t, dst[t], c + t, src[t])

    def body(j, carry):
        c = ahead * j
        lax.cond(j % 2 == 0, lambda: step(c, sets[0], sets[1]), lambda: step(c, sets[1], sets[0]))
        return carry

    if n_it:
        lax.fori_loop(0, n_it, body, 0)
    c = ahead * n_it
    src, dst = (sets[0], sets[1]) if n_it % 2 == 0 else (sets[1], sets[0])
    rem = n_chunks - c
    for t in range(min(rem, ahead)):
        both(c + ahead + t, dst[t] if c + ahead + t < n_chunks else None, c + t, src[t])
    for t in range(max(rem - ahead, 0)):
        both(None, None, c + ahead + t, dst[t])
    acc = acc_scr[...]
    o_t = acc[:V_DIM] / acc[V_DIM:V_DIM + 1]
    o_ref[...] = (o_t.T * _silu(ga_ref[...])).astype(o_ref.dtype)


def _attn_kernel_aliased(q_ref, k_ref, vt_ref, ga_ref, prev_ref, o_ref, *scratch, **kw):
    del prev_ref
    _attn_kernel(q_ref, k_ref, vt_ref, ga_ref, o_ref, *scratch, **kw)


def _attention(q, k, vt, p, *, tq, q_row0, q_rows, key_row0, key_rows, prev=None):
    bsz, rows, _ = q.shape
    assert q_row0 % tq == 0 and q_rows % tq == 0 and key_row0 % key_rows == 0
    qb, kb = q_row0 // tq, key_row0 // key_rows
    assert tq % min(tq, ATT_QW) == 0
    tk = _mxu_tile(key_rows, ATT_TK)
    kw = dict(n_chunks=key_rows // tk, tk=tk)
    in_specs = [pl.BlockSpec((None, tq, QK_PAD), lambda b, h, i: (b, qb + i, h)),
                pl.BlockSpec((None, key_rows, QK_PAD), lambda b, h, i: (b, kb, h)),
                pl.BlockSpec((None, None, VT_ROWS, key_rows), lambda b, h, i: (b, h, 0, kb)),
                pl.BlockSpec((None, tq, V_DIM), lambda b, h, i: (b, qb + i, OFF_GA // V_DIM + h))]
    args = [q, k, vt, p]
    if prev is None:
        kern, aliases, name = functools.partial(_attn_kernel, **kw), {}, "attention"
    else:
        kern, aliases, name = functools.partial(_attn_kernel_aliased, **kw), {4: 0}, "attention_ctx"
        in_specs.append(pl.BlockSpec(memory_space=pl.ANY))
        args.append(prev)
    return pl.pallas_call(
        kern,
        grid=(bsz, MLA_HEADS, q_rows // tq),
        in_specs=in_specs,
        out_specs=pl.BlockSpec((None, tq, V_DIM), lambda b, h, i: (b, qb + i, h)),
        out_shape=jax.ShapeDtypeStruct((bsz, rows, MLA_WIDTH), MXU_DTYPE),
        scratch_shapes=[pltpu.VMEM((1, tq), F32), pltpu.VMEM((VT_ROWS, tq), F32),
                        pltpu.VMEM((2 * ATT_AHEAD, 1, tq), F32),
                        *[pltpu.VMEM((tk, tq), F32) for _ in range(2 * ATT_AHEAD)]],
        input_output_aliases=aliases,
        compiler_params=_params("parallel", "parallel", "arbitrary"),
        name=name,
    )(*args)


CONV_TC = 2048
CONV_TR = 2 * CHUNK
HALO = SUBLANES


def _conv_kernel(prev_ref, cur_ref, next_ref, w_ref, b_ref, o_ref, ext, *, lat_tiles, all_tiles):
    c = pl.program_id(1)
    has_left = jnp.logical_and(c != 0, c != lat_tiles)
    has_right = jnp.logical_and(c != lat_tiles - 1, c != all_tiles - 1)
    ext[0:HALO, :] = jnp.where(has_left, prev_ref[...], 0.0)
    ext[HALO:HALO + CONV_TR, :] = cur_ref[...]
    ext[HALO + CONV_TR:, :] = jnp.where(has_right, next_ref[...], 0.0)
    pad = (CONV_K - 1) // 2
    acc = jnp.broadcast_to(b_ref[...], (CONV_TR, CONV_TC))
    for kk in range(CONV_K):
        lo = HALO - pad + kk
        acc = acc + w_ref[kk:kk + 1, :] * ext[lo:lo + CONV_TR, :]
    o_ref[...] = _silu(acc)


def _conv(p, w8, bias, n_lat):
    bsz, rows, _ = p.shape
    all_tiles = rows // CONV_TR
    lat_tiles = n_lat // CONV_TR
    per = CONV_TR // HALO
    cb = OFF_X // CONV_TC
    last_halo = rows // HALO - 1
    kern = functools.partial(_conv_kernel, lat_tiles=lat_tiles, all_tiles=all_tiles)
    return pl.pallas_call(
        kern,
        grid=(bsz, all_tiles, CONV_CH // CONV_TC),
        in_specs=[pl.BlockSpec((None, HALO, CONV_TC),
                               lambda b, c, j: (b, jnp.maximum(c * per - 1, 0), cb + j)),
                  pl.BlockSpec((None, CONV_TR, CONV_TC), lambda b, c, j: (b, c, cb + j)),
                  pl.BlockSpec((None, HALO, CONV_TC),
                               lambda b, c, j: (b, jnp.minimum((c + 1) * per, last_halo), cb + j)),
                  pl.BlockSpec((SUBLANES, CONV_TC), lambda b, c, j: (0, j)),
                  pl.BlockSpec((1, CONV_TC), lambda b, c, j: (0, j))],
        out_specs=pl.BlockSpec((None, CONV_TR, CONV_TC), lambda b, c, j: (b, c, j)),
        out_shape=jax.ShapeDtypeStruct((bsz, rows, CONV_CH), F32),
        scratch_shapes=[pltpu.VMEM((CONV_TR + 2 * HALO, CONV_TC), F32)],
        compiler_params=_params("parallel", "parallel", "arbitrary"),
        name="dwconv",
    )(p, p, p, w8, bias.reshape(1, CONV_CH))


def _ssd_kernel(*refs, reverse):
    if reverse:
        (x_ref, b_ref, c_ref, dt_ref, bias_ref, alog_ref, e2_ref,
         yf_ref, z_ref, dsk_ref, gs_ref, o_ref, state, rrow, ybuf) = refs
        dest = ybuf
    else:
        (x_ref, b_ref, c_ref, dt_ref, bias_ref, alog_ref, e2_ref, o_ref, state, rrow) = refs
        dest = o_ref
    L = CHUNK
    half = PAIR_W // 2

    @pl.when(pl.program_id(1) == 0)
    def _():
        state[...] = jnp.zeros_like(state)

    off = SSM_HEADS if reverse else 0
    dt = _softplus(dt_ref[:, off:off + SSM_HEADS] + bias_ref[...])
    a = dt * (-jnp.exp(alog_ref[...]))

    ri = lax.broadcasted_iota(jnp.int32, (L, L), 0)
    ci = lax.broadcasted_iota(jnp.int32, (L, L), 1)
    tri = jnp.where((ri <= ci) if reverse else (ri >= ci), 1.0, 0.0).astype(MXU_DTYPE)
    p1 = a.astype(MXU_DTYPE)
    r1 = a - p1.astype(F32)
    p2 = r1.astype(MXU_DTYPE)
    p3 = (r1 - p2.astype(F32)).astype(MXU_DTYPE)
    acs = _dot(tri, p1) + _dot(tri, p2) + _dot(tri, p3)

    acs_t = acs.T
    lane_h = lax.broadcasted_iota(jnp.int32, (SSM_HEADS // 2, LANES), 1)
    ev = acs_t[0:SSM_HEADS // 2]
    od = acs_t[SSM_HEADS // 2:]
    rrow[0] = jnp.where(lane_h < half, ev, pltpu.roll(od, half, 1))
    rrow[1] = jnp.where(lane_h < half, pltpu.roll(ev, half, 1), od)

    acs_hl = jnp.concatenate(_split_hi_lo(acs), axis=1)
    dt_hl = jnp.concatenate(_split_hi_lo(dt), axis=1)
    edge = 0 if reverse else L - 1
    far_block = 0 if reverse else 1

    li = lax.broadcasted_iota(jnp.int32, (L, LANES), 0)
    lane = lax.broadcasted_iota(jnp.int32, (L, LANES), 1)
    src = jnp.where(lane < half, lane, lane - half)
    masks = []
    for sb in range(2):
        s_pos = src + sb * half
        masks.append((li <= s_pos) if reverse else (li >= s_pos))
    lane_s = lax.broadcasted_iota(jnp.int32, (half, LANES), 1)
    first_head = lane_s < half

    def group(g, carry):
        c0 = pl.multiple_of(g * GROUP_W, GROUP_W)
        n0 = pl.multiple_of(g * SSM_STATE, SSM_STATE)
        e2g = e2_ref[:, pl.ds(c0, GROUP_W)]
        acs_x = _dot(acs_hl, e2g)
        dt_x = _dot(dt_hl, e2g)
        xdt = x_ref[:, pl.ds(c0, GROUP_W)] * dt_x
        bg = b_ref[:, pl.ds(n0, SSM_STATE)]
        cg = c_ref[:, pl.ds(n0, SSM_STATE)].astype(MXU_DTYPE)
        bgb = bg.astype(MXU_DTYPE)
        last_x = acs_x[edge:edge + 1, :]

        sg = state[g]
        y_off = _dot(cg, sg.astype(MXU_DTYPE)) * jnp.exp(acs_x)
        wgt = (xdt * jnp.exp(last_x - acs_x)).astype(MXU_DTYPE)
        state[g] = sg * jnp.exp(last_x) + _dot(bg.T.astype(MXU_DTYPE), wgt)

        cbd = []
        for sb in range(2):
            bs = bgb[sb * half:(sb + 1) * half]
            cbd.append(_dot_nt(cg, jnp.concatenate([bs, bs], axis=0)))

        for jj in range(GROUP_W // PAIR_W):
            col = acs_x[:, jj * PAIR_W:(jj + 1) * PAIR_W]
            yp = y_off[:, jj * PAIR_W:(jj + 1) * PAIR_W]
            for sb in range(2):
                row = rrow[sb, pl.ds(g * (GROUP_W // PAIR_W) + jj, 1), :]
                xs = xdt[sb * half:(sb + 1) * half, jj * PAIR_W:(jj + 1) * PAIR_W]
                xbd = jnp.concatenate([jnp.where(first_head, xs, 0.0),
                                       jnp.where(first_head, 0.0, xs)], axis=0).astype(MXU_DTYPE)
                if sb == far_block:
                    t0 = sb * half
                    seg = jnp.where(masks[sb][t0:t0 + half], col[t0:t0 + half] - row, -jnp.inf)
                    mp = (cbd[sb][t0:t0 + half] * jnp.exp(seg)).astype(MXU_DTYPE)
                    upd = _dot(mp, xbd)
                    parts = [yp[:half], yp[half:]]
                    parts[sb] = parts[sb] + upd
                    yp = jnp.concatenate(parts, axis=0)
                else:
                    lm = jnp.exp(jnp.where(masks[sb], col - row, -jnp.inf))
                    mp = (cbd[sb] * lm).astype(MXU_DTYPE)
                    yp = yp + _dot(mp, xbd)
            dest[:, pl.ds(pl.multiple_of(c0 + jj * PAIR_W, PAIR_W), PAIR_W)] = yp
        return carry

    lax.fori_loop(0, SSM_GROUPS, group, 0, unroll=2)

    if reverse:
        ssq = jnp.zeros((L, 1), F32)
        for g in range(SSM_GROUPS):
            sl = slice(g * GROUP_W, (g + 1) * GROUP_W)
            y = ybuf[:, sl] + yf_ref[:, sl] + x_ref[:, sl] * dsk_ref[:, sl]
            yz = y * _silu(z_ref[:, sl])
            ybuf[:, sl] = yz
            ssq = ssq + jnp.sum(yz * yz, axis=-1, keepdims=True)
        r = lax.rsqrt(ssq / SSM_INNER + EPS)
        for g in range(SSM_GROUPS):
            sl = slice(g * GROUP_W, (g + 1) * GROUP_W)
            o_ref[:, sl] = (ybuf[:, sl] * r * gs_ref[:, sl]).astype(o_ref.dtype)


def _ssd(xbc, p, bias, alog, e2, n_lat, *, reverse, yf=None, dskip_x=None, g_ssm=None):
    bsz, rows, _ = xbc.shape
    all_chunks = rows // CHUNK
    lat_chunks = n_lat // CHUNK
    ctx_chunks = all_chunks - lat_chunks

    if reverse:
        def chunk(s):
            return all_chunks - 1 - s
    else:
        def chunk(s):
            return jnp.where(s < ctx_chunks, lat_chunks + s, s - ctx_chunks)

    row_spec = lambda width, blk: pl.BlockSpec((None, CHUNK, width), lambda b, s: (b, chunk(s), blk))
    const = lambda shape: pl.BlockSpec(shape, lambda b, s: (0,) * len(shape))
    ng = SSM_GROUPS * SSM_STATE
    in_specs = [row_spec(SSM_INNER, 0),
                row_spec(ng, SSM_INNER // ng),
                row_spec(ng, SSM_INNER // ng + 1),
                row_spec(2 * SSM_HEADS, OFF_DT // (2 * SSM_HEADS)),
                const((1, SSM_HEADS)), const((1, SSM_HEADS)), const((2 * SSM_HEADS, SSM_INNER))]
    args = [xbc, xbc, xbc, p, bias.reshape(1, SSM_HEADS), alog.reshape(1, SSM_HEADS), e2]
    scratch = [pltpu.VMEM((SSM_GROUPS, SSM_STATE, GROUP_W), F32),
               pltpu.VMEM((2, SSM_HEADS // 2, LANES), F32)]
    if reverse:
        in_specs += [row_spec(SSM_INNER, 0), row_spec(SSM_INNER, OFF_Z // SSM_INNER),
                     const((1, SSM_INNER)), const((1, SSM_INNER))]
        args += [yf, p, dskip_x.reshape(1, SSM_INNER), g_ssm.reshape(1, SSM_INNER)]
        scratch.append(pltpu.VMEM((CHUNK, SSM_INNER), F32))
        out_dtype = MXU_DTYPE
    else:
        out_dtype = F32
    return pl.pallas_call(
        functools.partial(_ssd_kernel, reverse=reverse),
        grid=(bsz, all_chunks),
        in_specs=in_specs,
        out_specs=row_spec(SSM_INNER, 0),
        out_shape=jax.ShapeDtypeStruct((bsz, rows, SSM_INNER), out_dtype),
        scratch_shapes=scratch,
        compiler_params=_params("parallel", "arbitrary"),
        name="ssd_bwd" if reverse else "ssd_fwd",
    )(*args)


def _head_perm():
    k = jnp.arange(SSM_HEADS)
    return jnp.where(k < SSM_HEADS // 2, 2 * k, 2 * (k - SSM_HEADS // 2) + 1)


def _prep_w_in(w):
    sizes = (Q_RANK, KV_RANK, QK_ROPE, MLA_WIDTH, SSM_INNER, SSM_INNER,
             SSM_GROUPS * SSM_STATE, SSM_GROUPS * SSM_STATE, SSM_HEADS, SSM_HEADS, D_MODEL, D_MODEL)
    offs = [0]
    for s in sizes:
        offs.append(offs[-1] + s)
    (cq, ckv, kpe, ga, z, xs, bs, cs, dtf, dtb, mga, mgb) = [w[:, offs[i]:offs[i + 1]] for i in range(12)]
    perm = _head_perm()
    zpad = jnp.zeros((w.shape[0], LANES - QK_ROPE), w.dtype)
    k1, k2 = kpe[:, :QK_ROPE // 2], kpe[:, QK_ROPE // 2:]
    cols = [z, xs, bs, cs, ga, mga, mgb, ckv, dtf[:, perm], dtb[:, perm], cq,
            kpe, zpad, -k2, k1, zpad]
    used = sum(c.shape[1] for c in cols)
    cols.append(jnp.zeros((w.shape[0], IN_PAD - used), w.dtype))
    return jnp.concatenate(cols, axis=1).astype(MXU_DTYPE)


def _prep_w_uq(w):
    w3 = w.reshape(Q_RANK, MLA_HEADS, QK_DIM)
    nope = w3[..., :QK_NOPE]
    r1 = w3[..., QK_NOPE:QK_NOPE + QK_ROPE // 2]
    r2 = w3[..., QK_NOPE + QK_ROPE // 2:]
    zp = jnp.zeros((Q_RANK, MLA_HEADS, LANES - QK_ROPE), w.dtype)
    wa = jnp.concatenate([nope, r1, r2, zp], axis=-1).reshape(Q_RANK, MLA_HEADS * QK_PAD)
    wb = jnp.concatenate([-r2, r1, zp], axis=-1).reshape(Q_RANK, MLA_HEADS * LANES)
    return wa.astype(MXU_DTYPE), wb.astype(MXU_DTYPE)


def _prep_w_ukv(w):
    w3 = w.reshape(KV_RANK, MLA_HEADS, QK_NOPE + V_DIM)
    wk = w3[..., :QK_NOPE].reshape(KV_RANK, MLA_HEADS * QK_NOPE)
    wvt = w3[..., QK_NOPE:].reshape(KV_RANK, MLA_HEADS * V_DIM).T
    return wk.astype(MXU_DTYPE), wvt.astype(MXU_DTYPE)


def _rope_tables(n_lat, n_ctx):
    n_rows = n_lat // GRID_W
    rows = jnp.broadcast_to(jnp.arange(n_rows, dtype=F32)[:, None], (n_rows, GRID_W)).reshape(-1)
    cols = jnp.broadcast_to(jnp.arange(GRID_W, dtype=F32)[None, :], (n_rows, GRID_W)).reshape(-1)
    n_freq = QK_ROPE // 4
    inv = ROPE_BASE ** (-jnp.arange(n_freq, dtype=F32) / n_freq)
    ang = jnp.concatenate([rows[:, None] * inv, cols[:, None] * inv], axis=-1)
    cos, sin = jnp.cos(ang), jnp.sin(ang)
    ones = jnp.ones((n_lat, LANES - QK_ROPE), F32)
    cos_t = jnp.concatenate([cos, cos, ones], axis=1)
    sin_t = jnp.concatenate([sin, sin, 0.0 * ones], axis=1)
    cos_t = jnp.concatenate([cos_t, jnp.ones((n_ctx, LANES), F32)], axis=0)
    sin_t = jnp.concatenate([sin_t, jnp.zeros((n_ctx, LANES), F32)], axis=0)
    return cos_t, sin_t


def _expand_matrix():
    head = _head_perm()[:, None]
    col_head = (jnp.arange(SSM_INNER) // SSM_HEADDIM)[None, :]
    e = jnp.where(head == col_head, 1.0, 0.0).astype(MXU_DTYPE)
    return jnp.concatenate([e, e], axis=0)


def kernel(x, c, ctx, c_ctx, w_ada, b_ada, g_pre, w_in, g_q, w_uq, g_kv, w_ukv, conv_w, conv_b,
           dt_bias_f, dt_bias_b, a_log_f, a_log_b, d_skip, g_ssm, w_proj_a, w_proj_b, w_out, g_final):
    bsz, n_lat, d = x.shape
    n_ctx = ctx.shape[1]
    depth = w_in.shape[0]
    rows = n_lat + n_ctx
    assert d == D_MODEL and n_lat % GRID_W == 0 and n_lat % n_ctx == 0 and n_ctx % (2 * CHUNK) == 0
    assert bsz + 1 <= SUBLANES

    hs = jnp.concatenate([x, ctx], axis=1)
    cc = jnp.zeros((SUBLANES, d), F32).at[:bsz].set(c).at[bsz].set(c_ctx)
    cos_t, sin_t = _rope_tables(n_lat, n_ctx)
    e2 = _expand_matrix()
    perm = _head_perm()
    tm_all = _row_tile(rows, 1056)

    for i in range(depth):
        last = i == depth - 1
        out_rows = n_lat if last else rows
        tm_out = _row_tile(out_rows, 1056)
        tm_half = _row_tile(out_rows, 528)

        w_in_r = _prep_w_in(w_in[i])
        wqa, wqb = _prep_w_uq(w_uq[i])
        wk, wvt = _prep_w_ukv(w_ukv[i])
        wa = _cast_layer(w_proj_a, i)
        wb = _cast_layer(w_proj_b, i)
        wo = _cast_layer(w_out, i)
        conv_w8 = jnp.zeros((SUBLANES, CONV_CH), F32).at[:CONV_K].set(conv_w[i])

        mod = _ada(cc, w_ada[i], b_ada[i])
        mod3 = mod.reshape(SUBLANES, 1, 3 * d)
        u = _norm_mod(hs, g_pre[i], mod3, n_lat)
        p = _matmul(u, w_in_r, rows=rows, tm=tm_all, tn=1024, out_dtype=F32, name="in_proj")

        q = _q_proj(p, g_q[i], wqa, wqb, cos_t, sin_t, tm_all)
        k, vt = _kv_proj(p, g_kv[i], wk, wvt, cos_t, sin_t, _lane_tile(rows, 1024))
        oa = _attention(q, k, vt, p, tq=_mxu_tile(n_lat, ATT_TQ), q_row0=0, q_rows=n_lat,
                        key_row0=0, key_rows=rows)
        if not last:
            oa = _attention(q, k, vt, p, tq=_mxu_tile(n_ctx, ATT_TQ_CTX), q_row0=n_lat, q_rows=n_ctx,
                            key_row0=n_lat, key_rows=n_ctx, prev=oa)
        bra = _matmul(oa, wa, rows=out_rows, tm=tm_out, tn=1024, out_dtype=F32, name="proj_a")

        xbc = _conv(p, conv_w8, conv_b[i], n_lat)
        yf = _ssd(xbc, p, dt_bias_f[i][perm], a_log_f[i][perm], e2, n_lat, reverse=False)
        yn = _ssd(xbc, p, dt_bias_b[i][perm], a_log_b[i][perm], e2, n_lat, reverse=True,
                  yf=yf, dskip_x=jnp.repeat(d_skip[i], SSM_HEADDIM), g_ssm=g_ssm[i])

        tn_m = 512
        tile = lambda blk0: pl.BlockSpec((None, tm_half, tn_m), lambda b, r, j: (b, r, blk0 + j))
        merged = _matmul(
            yn, wb, rows=out_rows, tm=tm_half, tn=tn_m, out_dtype=MXU_DTYPE,
            epilogue=_merge_epilogue,
            extras=[(bra, tile(0)), (p, tile(OFF_MGA // tn_m)), (p, tile(OFF_MGB // tn_m))],
            name="proj_b_merge")
        gate_spec = lambda row: pl.BlockSpec((None, 1, tn_m), lambda b, r, j: (row(b), 0, 2 * (d // tn_m) + j))
        hs = _matmul(
            merged, wo, rows=out_rows, tm=tm_out, tn=tn_m, out_dtype=F32,
            epilogue=functools.partial(_residual_epilogue, tm=tm_out, n_lat=n_lat),
            extras=[(hs, pl.BlockSpec((None, tm_out, tn_m), lambda b, r, j: (b, r, j))),
                    (mod3, gate_spec(lambda b: b)), (mod3, gate_spec(lambda b: bsz))],
            name="out_proj")

    return _final_norm(hs, g_final, n_lat)
```

```python
import functools
import math

import jax
import jax.numpy as jnp
from jax import lax
from jax.experimental import pallas as pl
from jax.experimental.pallas import tpu as pltpu

F32 = jnp.float32
MXU_DTYPE = jnp.bfloat16

D_MODEL = 4096
GRID_W = 64
MLA_HEADS = 32
QK_NOPE = 128
QK_ROPE = 64
V_DIM = 128
Q_RANK = 768
KV_RANK = 512
MLA_WIDTH = MLA_HEADS * V_DIM
QK_DIM = QK_NOPE + QK_ROPE
ATTN_SCALE = 1.0 / math.sqrt(QK_DIM)
Q_SCALE = ATTN_SCALE * math.log2(math.e)
ROPE_BASE = 10000.0
SSM_INNER = 2 * D_MODEL
SSM_HEADDIM = 64
SSM_HEADS = SSM_INNER // SSM_HEADDIM
SSM_GROUPS = 8
SSM_HPG = SSM_HEADS // SSM_GROUPS
SSM_STATE = 128
CONV_K = 5
CONV_CH = SSM_INNER + 2 * SSM_GROUPS * SSM_STATE
CHUNK = 128
EPS = 1e-6

LANES = 128
SUBLANES = 8
VMEM_LIMIT_BYTES = 56 * 1024 * 1024

QK_PAD = 2 * LANES
VT_ROWS = V_DIM + 16
GROUP_W = SSM_HPG * SSM_HEADDIM
PAIR_W = 2 * SSM_HEADDIM

OFF_Z = 0
OFF_X = OFF_Z + SSM_INNER
OFF_B = OFF_X + SSM_INNER
OFF_C = OFF_B + SSM_GROUPS * SSM_STATE
OFF_GA = OFF_X + CONV_CH
OFF_MGA = OFF_GA + MLA_WIDTH
OFF_MGB = OFF_MGA + D_MODEL
OFF_CKV = OFF_MGB + D_MODEL
OFF_DT = OFF_CKV + KV_RANK
OFF_CQ = OFF_DT + 2 * SSM_HEADS
OFF_KPE = OFF_CQ + Q_RANK
OFF_KPESW = OFF_KPE + LANES
IN_PAD = 32768
assert OFF_KPESW + LANES <= IN_PAD
assert OFF_CQ % Q_RANK == 0 and OFF_CKV % KV_RANK == 0 and OFF_DT % (2 * SSM_HEADS) == 0


def _sigmoid(x):
    return 1.0 / (1.0 + jnp.exp2(x * (-math.log2(math.e))))


def _silu(x):
    return x * _sigmoid(x)


def _softplus(x):
    return jnp.maximum(x, 0.0) + jnp.log1p(jnp.exp(-jnp.abs(x)))


def _dot(a, b):
    return jnp.dot(a, b, preferred_element_type=F32)


def _dot_nt(a, b):
    return lax.dot_general(a, b, (((1,), (1,)), ((), ())), preferred_element_type=F32)


def _split_hi_lo(x):
    hi = x.astype(MXU_DTYPE)
    lo = (x - hi.astype(F32)).astype(MXU_DTYPE)
    return hi, lo


def _row_tile(rows, target):
    best = None
    for t in range(16, min(rows, target) + 1, 16):
        if rows % t == 0:
            best = t
    assert best is not None, rows
    return best


def _lane_tile(rows, target):
    best = None
    for t in range(LANES, min(rows, target) + 1, LANES):
        if rows % t == 0:
            best = t
    assert best is not None, rows
    return best


def _params(*sem):
    return pltpu.CompilerParams(dimension_semantics=sem, vmem_limit_bytes=VMEM_LIMIT_BYTES)


def _cast_kernel(w_ref, o_ref):
    o_ref[...] = w_ref[...].astype(o_ref.dtype)


def _cast_layer(w_stack, layer):
    _, k, n = w_stack.shape
    tr = _row_tile(k, 512)
    return pl.pallas_call(
        _cast_kernel,
        grid=(k // tr,),
        in_specs=[pl.BlockSpec((None, tr, n), lambda r: (layer, r, 0))],
        out_specs=pl.BlockSpec((tr, n), lambda r: (r, 0)),
        out_shape=jax.ShapeDtypeStruct((k, n), MXU_DTYPE),
        compiler_params=_params("arbitrary"),
        name="cast_weight",
    )(w_stack)


def _ada_kernel(c_ref, w_ref, b_ref, o_ref):
    s = _silu(c_ref[...]).astype(MXU_DTYPE)
    o_ref[...] = _dot(s, w_ref[...].astype(MXU_DTYPE)) + b_ref[...]


def _ada(cc, w, b):
    rows, d = cc.shape
    n = w.shape[1]
    tn = 512
    return pl.pallas_call(
        _ada_kernel,
        grid=(n // tn,),
        in_specs=[pl.BlockSpec((rows, d), lambda j: (0, 0)),
                  pl.BlockSpec((d, tn), lambda j: (0, j)),
                  pl.BlockSpec((1, tn), lambda j: (0, j))],
        out_specs=pl.BlockSpec((rows, tn), lambda j: (0, j)),
        out_shape=jax.ShapeDtypeStruct((rows, n), F32),
        compiler_params=_params("arbitrary"),
        name="ada_mod",
    )(cc, w, b.reshape(1, n))


def _norm_mod_kernel(h_ref, g_ref, sh_ref, sc_ref, o_ref):
    x = h_ref[...]
    ms = jnp.mean(x * x, axis=-1, keepdims=True)
    y = x * lax.rsqrt(ms + EPS) * g_ref[...]
    o_ref[...] = (y * (1.0 + sc_ref[...]) + sh_ref[...]).astype(o_ref.dtype)


def _norm_mod(hs, g, mod3, n_lat):
    bsz, rows, d = hs.shape
    tr = 256
    lat_tiles = n_lat // tr
    ctx_row = bsz

    def mod_row(b, i):
        return jnp.where(i < lat_tiles, b, ctx_row)

    return pl.pallas_call(
        _norm_mod_kernel,
        grid=(bsz, rows // tr),
        in_specs=[pl.BlockSpec((None, tr, d), lambda b, i: (b, i, 0)),
                  pl.BlockSpec((1, d), lambda b, i: (0, 0)),
                  pl.BlockSpec((None, 1, d), lambda b, i: (mod_row(b, i), 0, 0)),
                  pl.BlockSpec((None, 1, d), lambda b, i: (mod_row(b, i), 0, 1))],
        out_specs=pl.BlockSpec((None, tr, d), lambda b, i: (b, i, 0)),
        out_shape=jax.ShapeDtypeStruct((bsz, rows, d), MXU_DTYPE),
        compiler_params=_params("parallel", "arbitrary"),
        name="norm_mod",
    )(hs, g.reshape(1, d), mod3, mod3)


def _final_norm_kernel(h_ref, g_ref, o_ref):
    x = h_ref[...]
    ms = jnp.mean(x * x, axis=-1, keepdims=True)
    o_ref[...] = x * lax.rsqrt(ms + EPS) * g_ref[...]


def _final_norm(hs, g, n_lat):
    bsz, _, d = hs.shape
    tr = 256
    return pl.pallas_call(
        _final_norm_kernel,
        grid=(bsz, n_lat // tr),
        in_specs=[pl.BlockSpec((None, tr, d), lambda b, i: (b, i, 0)),
                  pl.BlockSpec((1, d), lambda b, i: (0, 0))],
        out_specs=pl.BlockSpec((None, tr, d), lambda b, i: (b, i, 0)),
        out_shape=jax.ShapeDtypeStruct((bsz, n_lat, d), F32),
        compiler_params=_params("parallel", "arbitrary"),
        name="final_norm",
    )(hs, g.reshape(1, d))


def _mm_kernel(*refs, n_extra, epilogue):
    x_ref, w_ref = refs[:2]
    extras = refs[2:2 + n_extra]
    o_ref = refs[2 + n_extra]
    acc = _dot(x_ref[...], w_ref[...])
    if epilogue is not None:
        acc = epilogue(acc, *extras)
    o_ref[...] = acc.astype(o_ref.dtype)


def _matmul(x3, w, *, rows, tm, tn, out_dtype, epilogue=None, extras=(), name, single_buffer_x=False):
    bsz, r_all, k = x3.shape
    n = w.shape[1]
    assert rows % tm == 0 and n % tn == 0
    x_mode = dict(pipeline_mode=pl.Buffered(1)) if single_buffer_x else {}
    in_specs = [pl.BlockSpec((None, tm, k), lambda b, i, j: (b, i, 0), **x_mode),
                pl.BlockSpec((k, tn), lambda b, i, j: (0, j))]
    in_specs += [spec for _, spec in extras]
    return pl.pallas_call(
        functools.partial(_mm_kernel, n_extra=len(extras), epilogue=epilogue),
        grid=(bsz, rows // tm, n // tn),
        in_specs=in_specs,
        out_specs=pl.BlockSpec((None, tm, tn), lambda b, i, j: (b, i, j)),
        out_shape=jax.ShapeDtypeStruct((bsz, r_all, n), out_dtype),
        compiler_params=_params("parallel", "parallel", "arbitrary"),
        name=name,
    )(x3, w, *[a for a, _ in extras])


def _merge_epilogue(acc_b, bra_ref, mga_ref, mgb_ref):
    return _sigmoid(mga_ref[...]) * bra_ref[...] + _sigmoid(mgb_ref[...]) * acc_b


def _residual_epilogue(acc, h_ref, gate_ref, gate_c_ref, *, tm, n_lat):
    row = pl.program_id(1) * tm + lax.broadcasted_iota(jnp.int32, (tm, 1), 0)
    gate = jnp.where(row < n_lat, gate_ref[...], gate_c_ref[...])
    return h_ref[...] + gate * acc


Q_HEADS_PER_TILE = 4


def _q_kernel(cq_ref, g_ref, wa_ref, wb_ref, cos_ref, sin_ref, o_ref, xn_scr):
    @pl.when(pl.program_id(2) == 0)
    def _():
        x = cq_ref[...]
        ms = jnp.mean(x * x, axis=-1, keepdims=True)
        xn_scr[...] = (x * lax.rsqrt(ms + EPS) * g_ref[...]).astype(xn_scr.dtype)

    xn = xn_scr[...]
    a = _dot(xn, wa_ref[...])
    sw = _dot(xn, wb_ref[...])
    cos = cos_ref[...]
    sin = sin_ref[...]
    for hh in range(Q_HEADS_PER_TILE):
        base = hh * QK_PAD
        o_ref[:, base:base + LANES] = (a[:, base:base + LANES] * Q_SCALE).astype(o_ref.dtype)
        rope = a[:, base + LANES:base + QK_PAD] * cos + sw[:, hh * LANES:(hh + 1) * LANES] * sin
        o_ref[:, base + LANES:base + QK_PAD] = (rope * Q_SCALE).astype(o_ref.dtype)


def _q_proj(p, g_q, wqa, wqb, cos_t, sin_t, tm):
    bsz, rows, _ = p.shape
    tn = Q_HEADS_PER_TILE * QK_PAD
    n = MLA_HEADS * QK_PAD
    return pl.pallas_call(
        _q_kernel,
        grid=(bsz, rows // tm, n // tn),
        in_specs=[pl.BlockSpec((None, tm, Q_RANK), lambda b, i, j: (b, i, OFF_CQ // Q_RANK)),
                  pl.BlockSpec((1, Q_RANK), lambda b, i, j: (0, 0)),
                  pl.BlockSpec((Q_RANK, tn), lambda b, i, j: (0, j)),
                  pl.BlockSpec((Q_RANK, tn // 2), lambda b, i, j: (0, j)),
                  pl.BlockSpec((tm, LANES), lambda b, i, j: (i, 0)),
                  pl.BlockSpec((tm, LANES), lambda b, i, j: (i, 0))],
        out_specs=pl.BlockSpec((None, tm, tn), lambda b, i, j: (b, i, j)),
        out_shape=jax.ShapeDtypeStruct((bsz, rows, n), MXU_DTYPE),
        scratch_shapes=[pltpu.VMEM((tm, Q_RANK), MXU_DTYPE)],
        compiler_params=_params("parallel", "parallel", "arbitrary"),
        name="q_up",
    )(p, g_q.reshape(1, Q_RANK), wqa, wqb, cos_t, sin_t)


def _kv_kernel(ckv_ref, g_ref, wk_ref, wvt_ref, kpe_ref, kpesw_ref, cos_ref, sin_ref,
               k_ref, vt_ref, xn_scr, kp_scr):
    @pl.when(pl.program_id(2) == 0)
    def _():
        x = ckv_ref[...]
        ms = jnp.mean(x * x, axis=-1, keepdims=True)
        xn_scr[...] = (x * lax.rsqrt(ms + EPS) * g_ref[...]).astype(xn_scr.dtype)
        kp_scr[...] = (kpe_ref[...] * cos_ref[...] + kpesw_ref[...] * sin_ref[...]).astype(kp_scr.dtype)

    xn = xn_scr[...]
    kn = _dot(xn, wk_ref[...])
    vt = _dot_nt(wvt_ref[...], xn)
    sub = lax.broadcasted_iota(jnp.int32, (VT_ROWS - V_DIM, xn.shape[0]), 0)
    one_row = jnp.where(sub == 0, 1.0, 0.0).astype(vt_ref.dtype)
    for hh in range(Q_HEADS_PER_TILE):
        base = hh * QK_PAD
        k_ref[:, base:base + LANES] = kn[:, hh * LANES:(hh + 1) * LANES].astype(k_ref.dtype)
        k_ref[:, base + LANES:base + QK_PAD] = kp_scr[...]
        vt_ref[hh, 0:V_DIM, :] = vt[hh * V_DIM:(hh + 1) * V_DIM].astype(vt_ref.dtype)
        vt_ref[hh, V_DIM:VT_ROWS, :] = one_row


def _kv_proj(p, g_kv, wk, wvt, cos_t, sin_t, tm):
    bsz, rows, _ = p.shape
    assert rows % tm == 0 and tm % LANES == 0
    tk = Q_HEADS_PER_TILE * QK_PAD
    tw = Q_HEADS_PER_TILE * V_DIM
    nk = MLA_HEADS * QK_PAD
    return pl.pallas_call(
        _kv_kernel,
        grid=(bsz, rows // tm, nk // tk),
        in_specs=[pl.BlockSpec((None, tm, KV_RANK), lambda b, i, j: (b, i, OFF_CKV // KV_RANK)),
                  pl.BlockSpec((1, KV_RANK), lambda b, i, j: (0, 0)),
                  pl.BlockSpec((KV_RANK, tw), lambda b, i, j: (0, j)),
                  pl.BlockSpec((tw, KV_RANK), lambda b, i, j: (j, 0)),
                  pl.BlockSpec((None, tm, LANES), lambda b, i, j: (b, i, OFF_KPE // LANES)),
                  pl.BlockSpec((None, tm, LANES), lambda b, i, j: (b, i, OFF_KPESW // LANES)),
                  pl.BlockSpec((tm, LANES), lambda b, i, j: (i, 0)),
                  pl.BlockSpec((tm, LANES), lambda b, i, j: (i, 0))],
        out_specs=[pl.BlockSpec((None, tm, tk), lambda b, i, j: (b, i, j)),
                   pl.BlockSpec((None, Q_HEADS_PER_TILE, VT_ROWS, tm), lambda b, i, j: (b, j, 0, i))],
        out_shape=[jax.ShapeDtypeStruct((bsz, rows, nk), MXU_DTYPE),
                   jax.ShapeDtypeStruct((bsz, MLA_HEADS, VT_ROWS, rows), MXU_DTYPE)],
        scratch_shapes=[pltpu.VMEM((tm, KV_RANK), MXU_DTYPE), pltpu.VMEM((tm, LANES), MXU_DTYPE)],
        compiler_params=_params("parallel", "parallel", "arbitrary"),
        name="kv_up",
    )(p, g_kv.reshape(1, KV_RANK), wk, wvt, p, p, cos_t, sin_t)


ATT_TQ = 1024
ATT_TQ_CTX = 256
ATT_TK = 1024
ATT_QW = QK_PAD
ATT_AHEAD = 3


def _mxu_tile(n, limit):
    best = None
    for t in range(QK_PAD, min(n, limit) + 1, QK_PAD):
        if n % t == 0:
            best = t
    assert best is not None, n
    return best


def _attn_kernel(q_ref, k_ref, vt_ref, ga_ref, o_ref, m_scr, acc_scr, cmax_scr, *bufs, n_chunks, tk):
    tq = q_ref.shape[0]
    qw = min(tq, ATT_QW)
    m_scr[...] = jnp.full(m_scr.shape, -jnp.inf, F32)
    acc_scr[...] = jnp.zeros(acc_scr.shape, F32)
    slot = {id(b): i for i, b in enumerate(bufs)}

    def rows_of(c):
        r0 = c * tk
        return r0 if isinstance(r0, int) else pl.multiple_of(r0, tk)

    def scores(c, dst, n):
        cols = slice(n * qw, (n + 1) * qw)
        s = _dot_nt(k_ref[pl.ds(rows_of(c), tk), :], q_ref[cols, :])
        dst[:, cols] = s
        cmax_scr[slot[id(dst)], :, cols] = jnp.max(s, axis=0, keepdims=True)

    def absorb(c, src, n):
        cols = slice(n * qw, (n + 1) * qw)
        s = src[:, cols]
        m_old = m_scr[:, cols]
        m_new = jnp.maximum(m_old, cmax_scr[slot[id(src)], :, cols])
        p = jnp.exp2(s - m_new).astype(vt_ref.dtype)
        pv = _dot(vt_ref[:, pl.ds(rows_of(c), tk)], p)
        acc_scr[:, cols] = jnp.exp2(m_old - m_new) * acc_scr[:, cols] + pv
        m_scr[:, cols] = m_new

    def both(c_score, dst, c_absorb, src):
        for n in range(tq // qw):
            if dst is not None:
                scores(c_score, dst, n)
            if src is not None:
                absorb(c_absorb, src, n)

    ahead = len(bufs) // 2
    sets = (bufs[:ahead], bufs[ahead:])
    for t in range(min(ahead, n_chunks)):
        both(t, sets[0][t], None, None)
    n_it = max(n_chunks - ahead, 0) // ahead

    def step(c, src, dst):
        for t in range(ahead):
            both(c + ahead + t, dst[t], c + t, src[t])

    def body(j, carry):
        c = ahead * j
        lax.cond(j % 2 == 0, lambda: step(c, sets[0], sets[1]), lambda: step(c, sets[1], sets[0]))
        return carry

    if n_it:
        lax.fori_loop(0, n_it, body, 0)
    c = ahead * n_it
    src, dst = (sets[0], sets[1]) if n_it % 2 == 0 else (sets[1], sets[0])
    rem = n_chunks - c
    for t in range(min(rem, ahead)):
        both(c + ahead + t, dst[t] if c + ahead + t < n_chunks else None, c + t, src[t])
    for t in range(max(rem - ahead, 0)):
        both(None, None, c + ahead + t, dst[t])
    acc = acc_scr[...]
    o_t = acc[:V_DIM] / acc[V_DIM:V_DIM + 1]
    o_ref[...] = (o_t.T * _silu(ga_ref[...])).astype(o_ref.dtype)


def _attn_kernel_aliased(q_ref, k_ref, vt_ref, ga_ref, prev_ref, o_ref, *scratch, **kw):
    del prev_ref
    _attn_kernel(q_ref, k_ref, vt_ref, ga_ref, o_ref, *scratch, **kw)


def _attention(q, k, vt, p, *, tq, q_row0, q_rows, key_row0, key_rows, prev=None):
    bsz, rows, _ = q.shape
    assert q_row0 % tq == 0 and q_rows % tq == 0 and key_row0 % key_rows == 0
    qb, kb = q_row0 // tq, key_row0 // key_rows
    assert tq % min(tq, ATT_QW) == 0
    tk = _mxu_tile(key_rows, ATT_TK)
    kw = dict(n_chunks=key_rows // tk, tk=tk)
    in_specs = [pl.BlockSpec((None, tq, QK_PAD), lambda b, h, i: (b, qb + i, h)),
                pl.BlockSpec((None, key_rows, QK_PAD), lambda b, h, i: (b, kb, h)),
                pl.BlockSpec((None, None, VT_ROWS, key_rows), lambda b, h, i: (b, h, 0, kb)),
                pl.BlockSpec((None, tq, V_DIM), lambda b, h, i: (b, qb + i, OFF_GA // V_DIM + h))]
    args = [q, k, vt, p]
    if prev is None:
        kern, aliases, name = functools.partial(_attn_kernel, **kw), {}, "attention"
    else:
        kern, aliases, name = functools.partial(_attn_kernel_aliased, **kw), {4: 0}, "attention_ctx"
        in_specs.append(pl.BlockSpec(memory_space=pl.ANY))
        args.append(prev)
    return pl.pallas_call(
        kern,
        grid=(bsz, MLA_HEADS, q_rows // tq),
        in_specs=in_specs,
        out_specs=pl.BlockSpec((None, tq, V_DIM), lambda b, h, i: (b, qb + i, h)),
        out_shape=jax.ShapeDtypeStruct((bsz, rows, MLA_WIDTH), MXU_DTYPE),
        scratch_shapes=[pltpu.VMEM((1, tq), F32), pltpu.VMEM((VT_ROWS, tq), F32),
                        pltpu.VMEM((2 * ATT_AHEAD, 1, tq), F32),
                        *[pltpu.VMEM((tk, tq), F32) for _ in range(2 * ATT_AHEAD)]],
        input_output_aliases=aliases,
        compiler_params=_params("parallel", "parallel", "arbitrary"),
        name=name,
    )(*args)


CONV_TC = 2048
CONV_TR = 2 * CHUNK
HALO = SUBLANES


def _conv_kernel(prev_ref, cur_ref, next_ref, w_ref, b_ref, o_ref, ext, *, lat_tiles, all_tiles):
    c = pl.program_id(1)
    has_left = jnp.logical_and(c != 0, c != lat_tiles)
    has_right = jnp.logical_and(c != lat_tiles - 1, c != all_tiles - 1)
    ext[0:HALO, :] = jnp.where(has_left, prev_ref[...], 0.0)
    ext[HALO:HALO + CONV_TR, :] = cur_ref[...]
    ext[HALO + CONV_TR:, :] = jnp.where(has_right, next_ref[...], 0.0)
    pad = (CONV_K - 1) // 2
    acc = jnp.broadcast_to(b_ref[...], (CONV_TR, CONV_TC))
    for kk in range(CONV_K):
        lo = HALO - pad + kk
        acc = acc + w_ref[kk:kk + 1, :] * ext[lo:lo + CONV_TR, :]
    o_ref[...] = _silu(acc)


def _conv(p, w8, bias, n_lat):
    bsz, rows, _ = p.shape
    all_tiles = rows // CONV_TR
    lat_tiles = n_lat // CONV_TR
    per = CONV_TR // HALO
    cb = OFF_X // CONV_TC
    last_halo = rows // HALO - 1
    kern = functools.partial(_conv_kernel, lat_tiles=lat_tiles, all_tiles=all_tiles)
    return pl.pallas_call(
        kern,
        grid=(bsz, all_tiles, CONV_CH // CONV_TC),
        in_specs=[pl.BlockSpec((None, HALO, CONV_TC),
                               lambda b, c, j: (b, jnp.maximum(c * per - 1, 0), cb + j)),
                  pl.BlockSpec((None, CONV_TR, CONV_TC), lambda b, c, j: (b, c, cb + j)),
                  pl.BlockSpec((None, HALO, CONV_TC),
                               lambda b, c, j: (b, jnp.minimum((c + 1) * per, last_halo), cb + j)),
                  pl.BlockSpec((SUBLANES, CONV_TC), lambda b, c, j: (0, j)),
                  pl.BlockSpec((1, CONV_TC), lambda b, c, j: (0, j))],
        out_specs=pl.BlockSpec((None, CONV_TR, CONV_TC), lambda b, c, j: (b, c, j)),
        out_shape=jax.ShapeDtypeStruct((bsz, rows, CONV_CH), F32),
        scratch_shapes=[pltpu.VMEM((CONV_TR + 2 * HALO, CONV_TC), F32)],
        compiler_params=_params("parallel", "parallel", "arbitrary"),
        name="dwconv",
    )(p, p, p, w8, bias.reshape(1, CONV_CH))


def _ssd_kernel(*refs, reverse):
    if reverse:
        (x_ref, b_ref, c_ref, dt_ref, bias_ref, alog_ref, e2_ref,
         yf_ref, z_ref, gs_ref, o_ref, state, rrow, ybuf) = refs
        dest = ybuf
    else:
        (x_ref, b_ref, c_ref, dt_ref, bias_ref, alog_ref, e2_ref, dsk_ref,
         o_ref, state, rrow) = refs
        dest = o_ref
    L = CHUNK
    half = PAIR_W // 2

    @pl.when(pl.program_id(1) == 0)
    def _():
        state[...] = jnp.zeros_like(state)

    off = SSM_HEADS if reverse else 0
    dt = _softplus(dt_ref[:, off:off + SSM_HEADS] + bias_ref[...])
    a = dt * (-jnp.exp(alog_ref[...]))

    ri = lax.broadcasted_iota(jnp.int32, (L, L), 0)
    ci = lax.broadcasted_iota(jnp.int32, (L, L), 1)
    tri = jnp.where((ri <= ci) if reverse else (ri >= ci), 1.0, 0.0).astype(MXU_DTYPE)
    p1 = a.astype(MXU_DTYPE)
    r1 = a - p1.astype(F32)
    p2 = r1.astype(MXU_DTYPE)
    p3 = (r1 - p2.astype(F32)).astype(MXU_DTYPE)
    acs = _dot(tri, p1) + _dot(tri, p2) + _dot(tri, p3)

    acs_t = acs.T
    lane_h = lax.broadcasted_iota(jnp.int32, (SSM_HEADS // 2, LANES), 1)
    ev = acs_t[0:SSM_HEADS // 2]
    od = acs_t[SSM_HEADS // 2:]
    rrow[0] = jnp.where(lane_h < half, ev, pltpu.roll(od, half, 1))
    rrow[1] = jnp.where(lane_h < half, pltpu.roll(ev, half, 1), od)

    acs_hl = jnp.concatenate(_split_hi_lo(acs), axis=1)
    dt_hl = jnp.concatenate(_split_hi_lo(dt), axis=1)
    edge = 0 if reverse else L - 1
    far_block = 0 if reverse else 1

    li = lax.broadcasted_iota(jnp.int32, (L, LANES), 0)
    lane = lax.broadcasted_iota(jnp.int32, (L, LANES), 1)
    src = jnp.where(lane < half, lane, lane - half)
    masks = []
    for sb in range(2):
        s_pos = src + sb * half
        masks.append((li <= s_pos) if reverse else (li >= s_pos))
    lane_s = lax.broadcasted_iota(jnp.int32, (half, LANES), 1)
    first_head = lane_s < half

    def group(g, carry):
        c0 = pl.multiple_of(g * GROUP_W, GROUP_W)
        n0 = pl.multiple_of(g * SSM_STATE, SSM_STATE)
        e2g = e2_ref[:, pl.ds(c0, GROUP_W)]
        acs_x = _dot(acs_hl, e2g)
        dt_x = _dot(dt_hl, e2g)
        xdt = x_ref[:, pl.ds(c0, GROUP_W)] * dt_x
        bg = b_ref[:, pl.ds(n0, SSM_STATE)]
        cg = c_ref[:, pl.ds(n0, SSM_STATE)].astype(MXU_DTYPE)
        bgb = bg.astype(MXU_DTYPE)
        last_x = acs_x[edge:edge + 1, :]

        sg = state[g]
        y_off = _dot(cg, sg.astype(MXU_DTYPE)) * jnp.exp(acs_x)
        wgt = (xdt * jnp.exp(last_x - acs_x)).astype(MXU_DTYPE)
        state[g] = sg * jnp.exp(last_x) + _dot(bg.T.astype(MXU_DTYPE), wgt)

        cbd = []
        for sb in range(2):
            bs = bgb[sb * half:(sb + 1) * half]
            cbd.append(_dot_nt(cg, jnp.concatenate([bs, bs], axis=0)))

        for jj in range(GROUP_W // PAIR_W):
            col = acs_x[:, jj * PAIR_W:(jj + 1) * PAIR_W]
            yp = y_off[:, jj * PAIR_W:(jj + 1) * PAIR_W]
            for sb in range(2):
                row = rrow[sb, pl.ds(g * (GROUP_W // PAIR_W) + jj, 1), :]
                xs = xdt[sb * half:(sb + 1) * half, jj * PAIR_W:(jj + 1) * PAIR_W]
                xbd = jnp.concatenate([jnp.where(first_head, xs, 0.0),
                                       jnp.where(first_head, 0.0, xs)], axis=0).astype(MXU_DTYPE)
                if sb == far_block:
                    t0 = sb * half
                    seg = jnp.where(masks[sb][t0:t0 + half], col[t0:t0 + half] - row, -jnp.inf)
                    mp = (cbd[sb][t0:t0 + half] * jnp.exp(seg)).astype(MXU_DTYPE)
                    upd = _dot(mp, xbd)
                    parts = [yp[:half], yp[half:]]
                    parts[sb] = parts[sb] + upd
                    yp = jnp.concatenate(parts, axis=0)
                else:
                    lm = jnp.exp(jnp.where(masks[sb], col - row, -jnp.inf))
                    mp = (cbd[sb] * lm).astype(MXU_DTYPE)
                    yp = yp + _dot(mp, xbd)
            cols = pl.ds(pl.multiple_of(c0 + jj * PAIR_W, PAIR_W), PAIR_W)
            if not reverse:
                yp = yp + x_ref[:, cols] * dsk_ref[:, cols]
            dest[:, cols] = yp
        return carry

    lax.fori_loop(0, SSM_GROUPS, group, 0, unroll=4)

    if reverse:
        ssq = jnp.zeros((L, 1), F32)
        for g in range(SSM_GROUPS):
            sl = slice(g * GROUP_W, (g + 1) * GROUP_W)
            y = ybuf[:, sl] + yf_ref[:, sl]
            yz = y * _silu(z_ref[:, sl])
            ybuf[:, sl] = yz
            ssq = ssq + jnp.sum(yz * yz, axis=-1, keepdims=True)
        r = lax.rsqrt(ssq / SSM_INNER + EPS)
        for g in range(SSM_GROUPS):
            sl = slice(g * GROUP_W, (g + 1) * GROUP_W)
            o_ref[:, sl] = (ybuf[:, sl] * r * gs_ref[:, sl]).astype(o_ref.dtype)


def _ssd(xbc, p, bias, alog, e2, n_lat, *, reverse, yf=None, dskip_x=None, g_ssm=None):
    bsz, rows, _ = xbc.shape
    all_chunks = rows // CHUNK
    lat_chunks = n_lat // CHUNK
    ctx_chunks = all_chunks - lat_chunks

    if reverse:
        def chunk(s):
            return all_chunks - 1 - s
    else:
        def chunk(s):
            return jnp.where(s < ctx_chunks, lat_chunks + s, s - ctx_chunks)

    row_spec = lambda width, blk: pl.BlockSpec((None, CHUNK, width), lambda b, s: (b, chunk(s), blk))
    const = lambda shape: pl.BlockSpec(shape, lambda b, s: (0,) * len(shape))
    ng = SSM_GROUPS * SSM_STATE
    in_specs = [row_spec(SSM_INNER, 0),
                row_spec(ng, SSM_INNER // ng),
                row_spec(ng, SSM_INNER // ng + 1),
                row_spec(2 * SSM_HEADS, OFF_DT // (2 * SSM_HEADS)),
                const((1, SSM_HEADS)), const((1, SSM_HEADS)), const((2 * SSM_HEADS, SSM_INNER))]
    args = [xbc, xbc, xbc, p, bias.reshape(1, SSM_HEADS), alog.reshape(1, SSM_HEADS), e2]
    scratch = [pltpu.VMEM((SSM_GROUPS, SSM_STATE, GROUP_W), F32),
               pltpu.VMEM((2, SSM_HEADS // 2, LANES), F32)]
    if reverse:
        in_specs += [row_spec(SSM_INNER, 0), row_spec(SSM_INNER, OFF_Z // SSM_INNER),
                     const((1, SSM_INNER))]
        args += [yf, p, g_ssm.reshape(1, SSM_INNER)]
        scratch.append(pltpu.VMEM((CHUNK, SSM_INNER), F32))
        out_dtype = MXU_DTYPE
    else:
        in_specs.append(const((1, SSM_INNER)))
        args.append(dskip_x.reshape(1, SSM_INNER))
        out_dtype = F32
    return pl.pallas_call(
        functools.partial(_ssd_kernel, reverse=reverse),
        grid=(bsz, all_chunks),
        in_specs=in_specs,
        out_specs=row_spec(SSM_INNER, 0),
        out_shape=jax.ShapeDtypeStruct((bsz, rows, SSM_INNER), out_dtype),
        scratch_shapes=scratch,
        compiler_params=_params("parallel", "arbitrary"),
        name="ssd_bwd" if reverse else "ssd_fwd",
    )(*args)


def _head_perm():
    k = jnp.arange(SSM_HEADS)
    return jnp.where(k < SSM_HEADS // 2, 2 * k, 2 * (k - SSM_HEADS // 2) + 1)


def _prep_w_in(w):
    sizes = (Q_RANK, KV_RANK, QK_ROPE, MLA_WIDTH, SSM_INNER, SSM_INNER,
             SSM_GROUPS * SSM_STATE, SSM_GROUPS * SSM_STATE, SSM_HEADS, SSM_HEADS, D_MODEL, D_MODEL)
    offs = [0]
    for s in sizes:
        offs.append(offs[-1] + s)
    (cq, ckv, kpe, ga, z, xs, bs, cs, dtf, dtb, mga, mgb) = [w[:, offs[i]:offs[i + 1]] for i in range(12)]
    perm = _head_perm()
    zpad = jnp.zeros((w.shape[0], LANES - QK_ROPE), w.dtype)
    k1, k2 = kpe[:, :QK_ROPE // 2], kpe[:, QK_ROPE // 2:]
    cols = [z, xs, bs, cs, ga, mga, mgb, ckv, dtf[:, perm], dtb[:, perm], cq,
            kpe, zpad, -k2, k1, zpad]
    used = sum(c.shape[1] for c in cols)
    cols.append(jnp.zeros((w.shape[0], IN_PAD - used), w.dtype))
    return jnp.concatenate(cols, axis=1).astype(MXU_DTYPE)


def _prep_w_uq(w):
    w3 = w.reshape(Q_RANK, MLA_HEADS, QK_DIM)
    nope = w3[..., :QK_NOPE]
    r1 = w3[..., QK_NOPE:QK_NOPE + QK_ROPE // 2]
    r2 = w3[..., QK_NOPE + QK_ROPE // 2:]
    zp = jnp.zeros((Q_RANK, MLA_HEADS, LANES - QK_ROPE), w.dtype)
    wa = jnp.concatenate([nope, r1, r2, zp], axis=-1).reshape(Q_RANK, MLA_HEADS * QK_PAD)
    wb = jnp.concatenate([-r2, r1, zp], axis=-1).reshape(Q_RANK, MLA_HEADS * LANES)
    return wa.astype(MXU_DTYPE), wb.astype(MXU_DTYPE)


def _prep_w_ukv(w):
    w3 = w.reshape(KV_RANK, MLA_HEADS, QK_NOPE + V_DIM)
    wk = w3[..., :QK_NOPE].reshape(KV_RANK, MLA_HEADS * QK_NOPE)
    wvt = w3[..., QK_NOPE:].reshape(KV_RANK, MLA_HEADS * V_DIM).T
    return wk.astype(MXU_DTYPE), wvt.astype(MXU_DTYPE)


def _rope_tables(n_lat, n_ctx):
    n_rows = n_lat // GRID_W
    rows = jnp.broadcast_to(jnp.arange(n_rows, dtype=F32)[:, None], (n_rows, GRID_W)).reshape(-1)
    cols = jnp.broadcast_to(jnp.arange(GRID_W, dtype=F32)[None, :], (n_rows, GRID_W)).reshape(-1)
    n_freq = QK_ROPE // 4
    inv = ROPE_BASE ** (-jnp.arange(n_freq, dtype=F32) / n_freq)
    ang = jnp.concatenate([rows[:, None] * inv, cols[:, None] * inv], axis=-1)
    cos, sin = jnp.cos(ang), jnp.sin(ang)
    ones = jnp.ones((n_lat, LANES - QK_ROPE), F32)
    cos_t = jnp.concatenate([cos, cos, ones], axis=1)
    sin_t = jnp.concatenate([sin, sin, 0.0 * ones], axis=1)
    cos_t = jnp.concatenate([cos_t, jnp.ones((n_ctx, LANES), F32)], axis=0)
    sin_t = jnp.concatenate([sin_t, jnp.zeros((n_ctx, LANES), F32)], axis=0)
    return cos_t, sin_t


def _expand_matrix():
    head = _head_perm()[:, None]
    col_head = (jnp.arange(SSM_INNER) // SSM_HEADDIM)[None, :]
    e = jnp.where(head == col_head, 1.0, 0.0).astype(MXU_DTYPE)
    return jnp.concatenate([e, e], axis=0)


def kernel(x, c, ctx, c_ctx, w_ada, b_ada, g_pre, w_in, g_q, w_uq, g_kv, w_ukv, conv_w, conv_b,
           dt_bias_f, dt_bias_b, a_log_f, a_log_b, d_skip, g_ssm, w_proj_a, w_proj_b, w_out, g_final):
    bsz, n_lat, d = x.shape
    n_ctx = ctx.shape[1]
    depth = w_in.shape[0]
    rows = n_lat + n_ctx
    assert d == D_MODEL and n_lat % GRID_W == 0 and n_lat % n_ctx == 0 and n_ctx % (2 * CHUNK) == 0
    assert bsz + 1 <= SUBLANES

    hs = jnp.concatenate([x, ctx], axis=1)
    cc = jnp.zeros((SUBLANES, d), F32).at[:bsz].set(c).at[bsz].set(c_ctx)
    cos_t, sin_t = _rope_tables(n_lat, n_ctx)
    e2 = _expand_matrix()
    perm = _head_perm()
    tm_all = _row_tile(rows, 1056)

    for i in range(depth):
        last = i == depth - 1
        out_rows = n_lat if last else rows
        tm_out = _row_tile(out_rows, 1056)
        tm_half = _row_tile(out_rows, 528)

        w_in_r = _prep_w_in(w_in[i])
        wqa, wqb = _prep_w_uq(w_uq[i])
        wk, wvt = _prep_w_ukv(w_ukv[i])
        wa = _cast_layer(w_proj_a, i)
        wb = _cast_layer(w_proj_b, i)
        wo = _cast_layer(w_out, i)
        conv_w8 = jnp.zeros((SUBLANES, CONV_CH), F32).at[:CONV_K].set(conv_w[i])

        mod = _ada(cc, w_ada[i], b_ada[i])
        mod3 = mod.reshape(SUBLANES, 1, 3 * d)
        u = _norm_mod(hs, g_pre[i], mod3, n_lat)
        p = _matmul(u, w_in_r, rows=rows, tm=tm_all, tn=1024, out_dtype=F32, name="in_proj")

        q = _q_proj(p, g_q[i], wqa, wqb, cos_t, sin_t, tm_all)
        k, vt = _kv_proj(p, g_kv[i], wk, wvt, cos_t, sin_t, _lane_tile(rows, 1024))
        oa = _attention(q, k, vt, p, tq=_mxu_tile(n_lat, ATT_TQ), q_row0=0, q_rows=n_lat,
                        key_row0=0, key_rows=rows)
        if not last:
            oa = _attention(q, k, vt, p, tq=_mxu_tile(n_ctx, ATT_TQ_CTX), q_row0=n_lat, q_rows=n_ctx,
                            key_row0=n_lat, key_rows=n_ctx, prev=oa)
        bra = _matmul(oa, wa, rows=out_rows, tm=tm_out, tn=1024, out_dtype=F32, name="proj_a")

        xbc = _conv(p, conv_w8, conv_b[i], n_lat)
        yf = _ssd(xbc, p, dt_bias_f[i][perm], a_log_f[i][perm], e2, n_lat, reverse=False,
                  dskip_x=jnp.repeat(d_skip[i], SSM_HEADDIM))
        yn = _ssd(xbc, p, dt_bias_b[i][perm], a_log_b[i][perm], e2, n_lat, reverse=True,
                  yf=yf, g_ssm=g_ssm[i])

        tn_m = 512
        tile = lambda blk0: pl.BlockSpec((None, tm_out, tn_m), lambda b, r, j: (b, r, blk0 + j))
        merged = _matmul(
            yn, wb, rows=out_rows, tm=tm_out, tn=tn_m, out_dtype=MXU_DTYPE,
            epilogue=_merge_epilogue,
            extras=[(bra, tile(0)), (p, tile(OFF_MGA // tn_m)), (p, tile(OFF_MGB // tn_m))],
            name="proj_b_merge", single_buffer_x=True)
        gate_spec = lambda row: pl.BlockSpec((None, 1, tn_m), lambda b, r, j: (row(b), 0, 2 * (d // tn_m) + j))
        hs = _matmul(
            merged, wo, rows=out_rows, tm=tm_out, tn=tn_m, out_dtype=F32,
            epilogue=functools.partial(_residual_epilogue, tm=tm_out, n_lat=n_lat),
            extras=[(hs, pl.BlockSpec((None, tm_out, tn_m), lambda b, r, j: (b, r, j))),
                    (mod3, gate_spec(lambda b: b)), (mod3, gate_spec(lambda b: bsz))],
            name="out_proj")

    return _final_norm(hs, g_final, n_lat)
```
